```python
import jax, jax.numpy as jnp
from jax import lax
import numpy as np

D_MODEL = 4096
BATCH = 4
SEQ = 2048
DEPTH = 1

HEAD_DIM = 128
N_HEADS = D_MODEL // (2 * HEAD_DIM)
N_KV_HEADS = N_HEADS // 4
GQA_GROUP = N_HEADS // N_KV_HEADS
ATTN_WIDTH = N_HEADS * HEAD_DIM
KV_WIDTH = N_KV_HEADS * HEAD_DIM
WINDOW = 128
BLOCK = 128
ROPE_THETA = 10000.0
POOL_WIDTH = D_MODEL - ATTN_WIDTH
POOL_WINDOWS = (2, 4, 8, 16)
N_POOL_GROUPS = len(POOL_WINDOWS)
POOL_GROUP_WIDTH = POOL_WIDTH // N_POOL_GROUPS
MIX_WIDTH = ATTN_WIDTH + POOL_WIDTH
IN_PROJ_WIDTH = ATTN_WIDTH + 2 * KV_WIDTH + POOL_WIDTH
D_FF = ((8 * D_MODEL // 3 + 255) // 256) * 256
FFN_RES_WEIGHT = 0.5
RMS_EPS = 1e-6

kernel_name = "hymba_swa_sink_pool_macaron"


def rmsnorm(x, g):
    xf = x.astype(jnp.float32)
    y = xf * lax.rsqrt(jnp.mean(xf * xf, axis=-1, keepdims=True) + RMS_EPS)
    return (y * g.astype(jnp.float32)).astype(x.dtype)


def swiglu(h, w_gate, w_up, w_down):
    return (jax.nn.silu(h @ w_gate) * (h @ w_up)) @ w_down


def rope(t):
    s = t.shape[1]
    pos = jnp.arange(s, dtype=jnp.float32)
    inv_freq = ROPE_THETA ** (-jnp.arange(0, HEAD_DIM, 2, dtype=jnp.float32) / HEAD_DIM)
    ang = pos[:, None] * inv_freq[None, :]
    cos = jnp.cos(ang)[None, :, None, :]
    sin = jnp.sin(ang)[None, :, None, :]
    tf = t.astype(jnp.float32)
    t1, t2 = tf[..., : HEAD_DIM // 2], tf[..., HEAD_DIM // 2:]
    out = jnp.concatenate([t1 * cos - t2 * sin, t2 * cos + t1 * sin], axis=-1)
    return out.astype(t.dtype)


def sliding_window_attention(q, k, v, sinks):
    b, s = q.shape[0], q.shape[1]
    nb = s // BLOCK
    qb = q.reshape(b, nb, BLOCK, N_KV_HEADS, GQA_GROUP, HEAD_DIM)

    def with_prev(t):
        t = t.reshape(b, nb, BLOCK, N_KV_HEADS, HEAD_DIM)
        prev = jnp.concatenate([jnp.zeros_like(t[:, :1]), t[:, :-1]], axis=1)
        return jnp.concatenate([prev, t], axis=2)

    kb, vb = with_prev(k), with_prev(v)
    scale = HEAD_DIM ** -0.5
    sc = jnp.einsum('bnqhgd,bnkhd->bhgnqk', qb, kb,
                    preferred_element_type=jnp.float32) * scale
    qi = jnp.arange(BLOCK)[:, None]
    ki = jnp.arange(2 * BLOCK)[None, :]
    diff = qi + BLOCK - ki
    band = (diff >= 0) & (diff < WINDOW)
    blk = jnp.arange(nb)[:, None, None]
    valid = band[None] & ((blk > 0) | (ki >= BLOCK)[None])
    sc = jnp.where(valid[None, None, None], sc, -jnp.inf)
    sink = sinks.astype(jnp.float32).reshape(N_KV_HEADS, GQA_GROUP)[None, :, :, None, None, None]
    m = jnp.maximum(jnp.max(sc, axis=-1, keepdims=True), sink)
    e = jnp.exp(sc - m)
    denom = jnp.sum(e, axis=-1, keepdims=True) + jnp.exp(sink - m)
    p = (e / denom).astype(vb.dtype)
    out = jnp.einsum('bhgnqk,bnkhd->bnqhgd', p, vb)
    return out.reshape(b, s, ATTN_WIDTH)


def multiscale_pool(p, pool_w, pool_scale):
    b, s = p.shape[0], p.shape[1]
    pg = p.reshape(b, s, N_POOL_GROUPS, POOL_GROUP_WIDTH).astype(jnp.float32)
    c = jnp.cumsum(pg, axis=1)
    c0 = jnp.concatenate([jnp.zeros_like(c[:, :1]), c], axis=1)
    t = jnp.arange(s)
    outs = []
    for g, w in enumerate(POOL_WINDOWS):
        cg = c0[:, :, g, :]
        lo = jnp.maximum(t + 1 - w, 0)
        win_sum = cg[:, 1:] - jnp.take(cg, lo, axis=1)
        count = jnp.minimum(t + 1, w).astype(jnp.float32)[None, :, None]
        outs.append(win_sum / count)
    pooled = jnp.stack(outs, axis=2)
    y = (pooled - pg).astype(p.dtype)
    y = jnp.einsum('bsgc,gcd->bsgd', y, pool_w).reshape(b, s, POOL_WIDTH)
    return y * pool_scale


def setup_inputs(seed: int = 0) -> dict:
    key = jax.random.key(seed)
    ks = jax.random.split(key, 20)
    f32 = jnp.float32

    def w(k, shape, fan_in):
        return jax.random.normal(k, shape, f32) * fan_in ** -0.5

    def gain(k, shape):
        return 1.0 + 0.05 * jax.random.normal(k, shape, f32)

    L = DEPTH
    return {
        "x": jax.random.normal(ks[0], (BATCH, SEQ, D_MODEL), f32),
        "ffn1_pre_g": gain(ks[1], (L, D_MODEL)),
        "ffn1_w_gate": w(ks[2], (L, D_MODEL, D_FF), D_MODEL),
        "ffn1_w_up": w(ks[3], (L, D_MODEL, D_FF), D_MODEL),
        "ffn1_w_down": w(ks[4], (L, D_FF, D_MODEL), D_FF),
        "ffn1_post_g": gain(ks[5], (L, D_MODEL)),
        "mix_pre_g": gain(ks[6], (L, D_MODEL)),
        "w_in": w(ks[7], (L, D_MODEL, IN_PROJ_WIDTH), D_MODEL),
        "attn_sinks": jax.random.normal(ks[8], (L, N_HEADS), f32),
        "pool_w": w(ks[9], (L, N_POOL_GROUPS, POOL_GROUP_WIDTH, POOL_GROUP_WIDTH), POOL_GROUP_WIDTH),
        "pool_scale": gain(ks[10], (L, POOL_WIDTH)),
        "w_out": w(ks[11], (L, MIX_WIDTH, D_MODEL), MIX_WIDTH),
        "mix_post_g": gain(ks[12], (L, D_MODEL)),
        "ffn2_pre_g": gain(ks[13], (L, D_MODEL)),
        "ffn2_w_gate": w(ks[14], (L, D_MODEL, D_FF), D_MODEL),
        "ffn2_w_up": w(ks[15], (L, D_MODEL, D_FF), D_MODEL),
        "ffn2_w_down": w(ks[16], (L, D_FF, D_MODEL), D_FF),
        "ffn2_post_g": gain(ks[17], (L, D_MODEL)),
    }


def reference(x, ffn1_pre_g, ffn1_w_gate, ffn1_w_up, ffn1_w_down, ffn1_post_g,
              mix_pre_g, w_in, attn_sinks, pool_w, pool_scale, w_out, mix_post_g,
              ffn2_pre_g, ffn2_w_gate, ffn2_w_up, ffn2_w_down, ffn2_post_g):
    b, s = x.shape[0], x.shape[1]
    for l in range(DEPTH):
        h = swiglu(rmsnorm(x, ffn1_pre_g[l]), ffn1_w_gate[l], ffn1_w_up[l], ffn1_w_down[l])
        x = x + FFN_RES_WEIGHT * rmsnorm(h, ffn1_post_g[l])

        h = rmsnorm(x, mix_pre_g[l])
        proj = h @ w_in[l]
        q = proj[..., :ATTN_WIDTH]
        k = proj[..., ATTN_WIDTH:ATTN_WIDTH + KV_WIDTH]
        v = proj[..., ATTN_WIDTH + KV_WIDTH:ATTN_WIDTH + 2 * KV_WIDTH]
        p = proj[..., ATTN_WIDTH + 2 * KV_WIDTH:]
        q = rope(q.reshape(b, s, N_HEADS, HEAD_DIM))
        k = rope(k.reshape(b, s, N_KV_HEADS, HEAD_DIM))
        v = v.reshape(b, s, N_KV_HEADS, HEAD_DIM)
        a = sliding_window_attention(q, k, v, attn_sinks[l])
        pm = multiscale_pool(p, pool_w[l], pool_scale[l])
        y = jnp.concatenate([a, pm], axis=-1) @ w_out[l]
        x = x + rmsnorm(y, mix_post_g[l])

        h = swiglu(rmsnorm(x, ffn2_pre_g[l]), ffn2_w_gate[l], ffn2_w_up[l], ffn2_w_down[l])
        x = x + FFN_RES_WEIGHT * rmsnorm(h, ffn2_post_g[l])
    return x
```

```python
import functools

import jax
import jax.numpy as jnp
from jax import lax
from jax.experimental import pallas as pl
from jax.experimental.pallas import tpu as pltpu

D_MODEL = 4096
HEAD_DIM = 128
N_HEADS = 16
N_KV_HEADS = 4
GQA_GROUP = N_HEADS // N_KV_HEADS
ATTN_WIDTH = N_HEADS * HEAD_DIM
KV_WIDTH = N_KV_HEADS * HEAD_DIM
BLOCK = 128
ROPE_THETA = 10000.0
POOL_WIDTH = D_MODEL - ATTN_WIDTH
POOL_WINDOWS = (2, 4, 8, 16)
POOL_GROUP_WIDTH = POOL_WIDTH // len(POOL_WINDOWS)
IN_PROJ_WIDTH = ATTN_WIDTH + 2 * KV_WIDTH + POOL_WIDTH
FFN_RES_WEIGHT = 0.5
RMS_EPS = 1e-6
POOL_HALO = 16

BF16 = jnp.bfloat16
F32 = jnp.float32

VMEM_LIMIT_BYTES = 56 * 1024 * 1024


def _params(n_axes):
    return pltpu.CompilerParams(
        dimension_semantics=("arbitrary",) * n_axes,
        vmem_limit_bytes=VMEM_LIMIT_BYTES,
    )


def _rms(x, g):
    ms = jnp.mean(x * x, axis=-1, keepdims=True)
    return x * lax.rsqrt(ms + RMS_EPS) * g


def _norm_kernel(x_ref, g_ref, h_ref):
    h_ref[...] = _rms(x_ref[...], g_ref[...]).astype(h_ref.dtype)


def _prenorm(x, g, tm=512):
    m, d = x.shape
    return pl.pallas_call(
        _norm_kernel,
        grid=(m // tm,),
        in_specs=[pl.BlockSpec((tm, d), lambda i: (i, 0)),
                  pl.BlockSpec((1, d), lambda i: (0, 0))],
        out_specs=pl.BlockSpec((tm, d), lambda i: (i, 0)),
        out_shape=jax.ShapeDtypeStruct((m, d), BF16),
        compiler_params=_params(1),
        name="prenorm",
    )(x, g)


def _residual_kernel(x_ref, y_ref, gpost_ref, *rest, weight, with_next):
    if with_next:
        gnext_ref, xo_ref, h_ref = rest
    else:
        (xo_ref,) = rest
    r = _rms(y_ref[...], gpost_ref[...])
    if weight != 1.0:
        r = weight * r
    xn = x_ref[...] + r
    xo_ref[...] = xn
    if with_next:
        h_ref[...] = _rms(xn, gnext_ref[...]).astype(h_ref.dtype)


def _residual(x, y, g_post, g_next, weight, tm=256):
    m, d = x.shape
    with_next = g_next is not None
    row = pl.BlockSpec((tm, d), lambda i: (i, 0))
    vec = pl.BlockSpec((1, d), lambda i: (0, 0))
    in_specs = [row, row, vec] + ([vec] if with_next else [])
    args = (x, y, g_post) + ((g_next,) if with_next else ())
    out_specs = [row] + ([row] if with_next else [])
    out_shape = [jax.ShapeDtypeStruct((m, d), F32)] + (
        [jax.ShapeDtypeStruct((m, d), BF16)] if with_next else [])
    outs = pl.pallas_call(
        functools.partial(_residual_kernel, weight=weight, with_next=with_next),
        grid=(m // tm,),
        in_specs=in_specs,
        out_specs=out_specs,
        out_shape=out_shape,
        compiler_params=_params(1),
        name="residual_norm",
    )(*args)
    return (outs[0], outs[1]) if with_next else (outs[0], None)


def _gateup_kernel(h_ref, wg_ref, wu_ref, a_ref, wg_s, wu_s):
    @pl.when(pl.program_id(1) == 0)
    def _():
        wg_s[...] = wg_ref[...].astype(BF16)
        wu_s[...] = wu_ref[...].astype(BF16)

    h = h_ref[...]
    g = jnp.dot(h, wg_s[...], preferred_element_type=F32)
    u = jnp.dot(h, wu_s[...], preferred_element_type=F32)
    a_ref[...] = (jax.nn.silu(g) * u).astype(a_ref.dtype)


def _gateup(h, wg, wu, tm=512, tn=512):
    m, k = h.shape
    n = wg.shape[1]
    return pl.pallas_call(
        _gateup_kernel,
        grid=(pl.cdiv(n, tn), m // tm),
        in_specs=[pl.BlockSpec((tm, k), lambda j, i: (i, 0)),
                  pl.BlockSpec((k, tn), lambda j, i: (0, j)),
                  pl.BlockSpec((k, tn), lambda j, i: (0, j))],
        out_specs=pl.BlockSpec((tm, tn), lambda j, i: (i, j)),
        out_shape=jax.ShapeDtypeStruct((m, n), BF16),
        scratch_shapes=[pltpu.VMEM((k, tn), BF16), pltpu.VMEM((k, tn), BF16)],
        compiler_params=_params(2),
        name="ffn_gate_up",
    )(h, wg, wu)


def _down_kernel(a_ref, w_ref, y_ref):
    y_ref[...] = jnp.dot(a_ref[...], w_ref[...], preferred_element_type=F32)


def _down(a, wd_bf16, tm=512, tn=512):
    m, k = a.shape
    n = wd_bf16.shape[1]
    return pl.pallas_call(
        _down_kernel,
        grid=(n // tn, m // tm),
        in_specs=[pl.BlockSpec((tm, k), lambda j, i: (i, 0)),
                  pl.BlockSpec((k, tn), lambda j, i: (0, j))],
        out_specs=pl.BlockSpec((tm, tn), lambda j, i: (i, j)),
        out_shape=jax.ShapeDtypeStruct((m, n), F32),
        compiler_params=_params(2),
        name="ffn_down",
    )(a, wd_bf16)


def _proj_kernel(h_ref, w_ref, o_ref, w_s):
    @pl.when(pl.program_id(1) == 0)
    def _():
        w_s[...] = w_ref[...].astype(BF16)

    o_ref[...] = jnp.dot(h_ref[...], w_s[...], preferred_element_type=F32)


def _in_proj(h, w, tm=512, tn=512):
    m, k = h.shape
    n = w.shape[1]
    return pl.pallas_call(
        _proj_kernel,
        grid=(n // tn, m // tm),
        in_specs=[pl.BlockSpec((tm, k), lambda j, i: (i, 0)),
                  pl.BlockSpec((k, tn), lambda j, i: (0, j))],
        out_specs=pl.BlockSpec((tm, tn), lambda j, i: (i, j)),
        out_shape=jax.ShapeDtypeStruct((m, n), F32),
        scratch_shapes=[pltpu.VMEM((k, tn), BF16)],
        compiler_params=_params(2),
        name="mix_in_proj",
    )(h, w)


def _out_proj_kernel(a_ref, p_ref, wa_ref, wp_ref, o_ref, wa_s, wp_s):
    @pl.when(pl.program_id(1) == 0)
    def _():
        wa_s[...] = wa_ref[...].astype(BF16)
        wp_s[...] = wp_ref[...].astype(BF16)

    o_ref[...] = (jnp.dot(a_ref[...], wa_s[...], preferred_element_type=F32)
                  + jnp.dot(p_ref[...], wp_s[...], preferred_element_type=F32))


def _out_proj(a, pm, w, tm=512, tn=512):
    m, ka = a.shape
    kp = pm.shape[1]
    assert ka == kp and w.shape[0] == ka + kp
    n = w.shape[1]
    return pl.pallas_call(
        _out_proj_kernel,
        grid=(n // tn, m // tm),
        in_specs=[pl.BlockSpec((tm, ka), lambda j, i: (i, 0)),
                  pl.BlockSpec((tm, kp), lambda j, i: (i, 0)),
                  pl.BlockSpec((ka, tn), lambda j, i: (0, j)),
                  pl.BlockSpec((kp, tn), lambda j, i: (1, j))],
        out_specs=pl.BlockSpec((tm, tn), lambda j, i: (i, j)),
        out_shape=jax.ShapeDtypeStruct((m, n), F32),
        scratch_shapes=[pltpu.VMEM((ka, tn), BF16), pltpu.VMEM((kp, tn), BF16)],
        compiler_params=_params(2),
        name="mix_out_proj",
    )(a, pm, w, w)


def _rope(t, cos, sin_signed):
    return t * cos + pltpu.roll(t, HEAD_DIM // 2, 1) * sin_signed


def _attn_kernel(sinks_ref, q_ref, k_ref, v_ref, cos_ref, sin_ref, o_ref, k_s, v_s, *, seq):
    kvh = pl.program_id(1)
    g, blk = GQA_GROUP, BLOCK
    k_s[...] = _rope(k_ref[...], cos_ref[...], sin_ref[...]).astype(BF16)
    v_s[...] = v_ref[...].astype(BF16)
    sink = jnp.concatenate(
        [jnp.full((blk, 1), sinks_ref[kvh * g + c], F32) for c in range(g)], axis=0)
    row = lax.broadcasted_iota(jnp.int32, (g * blk, blk), 0) & (blk - 1)
    col = lax.broadcasted_iota(jnp.int32, (g * blk, blk), 1)
    cur_ok = col <= row
    scale = HEAD_DIM ** -0.5
    nt = (((1,), (1,)), ((), ()))

    def body(n, carry):
        r0 = pl.multiple_of(n * blk, blk)
        rp = pl.multiple_of(jnp.maximum(n - 1, 0) * blk, blk)
        cb = cos_ref[pl.ds(r0, blk), :]
        sb = sin_ref[pl.ds(r0, blk), :]
        q4 = jnp.concatenate(
            [_rope(q_ref[pl.ds(r0, blk), c * HEAD_DIM:(c + 1) * HEAD_DIM], cb, sb).astype(BF16)
             for c in range(g)], axis=0)
        s_c = lax.dot_general(q4, k_s[pl.ds(r0, blk), :], nt, preferred_element_type=F32) * scale
        s_p = lax.dot_general(q4, k_s[pl.ds(rp, blk), :], nt, preferred_element_type=F32) * scale
        prev_ok = col > row + jnp.where(n > 0, 0, blk)
        s_c = jnp.where(cur_ok, s_c, -jnp.inf)
        s_p = jnp.where(prev_ok, s_p, -jnp.inf)
        m = jnp.maximum(jnp.maximum(jnp.max(s_c, axis=1, keepdims=True),
                                    jnp.max(s_p, axis=1, keepdims=True)), sink)
        e_c = jnp.exp(s_c - m)
        e_p = jnp.exp(s_p - m)
        denom = (jnp.sum(e_c, axis=1, keepdims=True) + jnp.sum(e_p, axis=1, keepdims=True)
                 + jnp.exp(sink - m))
        inv = 1.0 / denom
        o = (jnp.dot((e_c * inv).astype(BF16), v_s[pl.ds(r0, blk), :], preferred_element_type=F32)
             + jnp.dot((e_p * inv).astype(BF16), v_s[pl.ds(rp, blk), :], preferred_element_type=F32))
        for c in range(g):
            o_ref[pl.ds(r0, blk), c * HEAD_DIM:(c + 1) * HEAD_DIM] = (
                o[c * blk:(c + 1) * blk].astype(o_ref.dtype))
        return carry

    lax.fori_loop(0, seq // blk, body, 0)


def _attention(proj, sinks, cos, sin_signed, batch, seq):
    m = proj.shape[0]
    qw = GQA_GROUP * HEAD_DIM
    k_blk0 = ATTN_WIDTH // HEAD_DIM
    v_blk0 = (ATTN_WIDTH + KV_WIDTH) // HEAD_DIM
    return pl.pallas_call(
        functools.partial(_attn_kernel, seq=seq),
        grid=(batch, N_KV_HEADS),
        in_specs=[pl.BlockSpec(memory_space=pltpu.SMEM),
                  pl.BlockSpec((seq, qw), lambda b, h: (b, h)),
                  pl.BlockSpec((seq, HEAD_DIM), lambda b, h: (b, k_blk0 + h)),
                  pl.BlockSpec((seq, HEAD_DIM), lambda b, h: (b, v_blk0 + h)),
                  pl.BlockSpec((seq, HEAD_DIM), lambda b, h: (0, 0)),
                  pl.BlockSpec((seq, HEAD_DIM), lambda b, h: (0, 0))],
        out_specs=pl.BlockSpec((seq, qw), lambda b, h: (b, h)),
        out_shape=jax.ShapeDtypeStruct((m, ATTN_WIDTH), BF16),
        scratch_shapes=[pltpu.VMEM((seq, HEAD_DIM), BF16), pltpu.VMEM((seq, HEAD_DIM), BF16)],
        compiler_params=_params(2),
        name="swa_attention",
    )(sinks, proj, proj, proj, cos, sin_signed)


def _pool_kernel(p_ref, w_ref, sc_ref, o_ref, pad_s, y_s, *, seq, rows):
    grp = pl.program_id(1)
    gw = POOL_GROUP_WIDTH
    pad_s[0:POOL_HALO, :] = jnp.zeros((POOL_HALO, gw), F32)
    pad_s[POOL_HALO:, :] = p_ref[...]
    ext = rows + POOL_HALO
    lrow = lax.broadcasted_iota(jnp.int32, (rows, 128), 0)

    for gi, w in enumerate(POOL_WINDOWS):
        assert w & (w - 1) == 0 and w - 1 <= POOL_HALO

        @pl.when(grp == gi)
        def _(w=w):
            def body(r, carry):
                r0 = pl.multiple_of(r * rows, rows)
                cnt = jnp.minimum(lrow + (r0 + 1), w).astype(F32)
                for c in range(gw // 128):
                    x = pad_s[pl.ds(r0, ext), c * 128:(c + 1) * 128]
                    s = x
                    d = 1
                    while d < w:
                        s = s + pltpu.roll(s, d, 0)
                        d *= 2
                    y = s[POOL_HALO:] / cnt - x[POOL_HALO:]
                    y_s[pl.ds(r0, rows), c * 128:(c + 1) * 128] = y.astype(BF16)
                return carry

            lax.fori_loop(0, seq // rows, body, 0)

    out = jnp.dot(y_s[...], w_ref[0].astype(BF16), preferred_element_type=F32) * sc_ref[...]
    o_ref[...] = out.astype(o_ref.dtype)


def _pool(proj, pool_w, pool_scale, batch, seq, rows=128):
    m = proj.shape[0]
    gw = POOL_GROUP_WIDTH
    ng = len(POOL_WINDOWS)
    p_blk0 = (ATTN_WIDTH + 2 * KV_WIDTH) // gw
    return pl.pallas_call(
        functools.partial(_pool_kernel, seq=seq, rows=rows),
        grid=(batch, ng),
        in_specs=[pl.BlockSpec((seq, gw), lambda b, g: (b, p_blk0 + g)),
                  pl.BlockSpec((1, gw, gw), lambda b, g: (g, 0, 0)),
                  pl.BlockSpec((1, gw), lambda b, g: (0, g))],
        out_specs=pl.BlockSpec((seq, gw), lambda b, g: (b, g)),
        out_shape=jax.ShapeDtypeStruct((m, POOL_WIDTH), BF16),
        scratch_shapes=[pltpu.VMEM((seq + POOL_HALO, gw), F32), pltpu.VMEM((seq, gw), BF16)],
        compiler_params=_params(2),
        name="multiscale_pool",
    )(proj, pool_w, pool_scale)


def _rope_tables(seq):
    pos = jnp.arange(seq, dtype=F32)
    inv_freq = ROPE_THETA ** (-jnp.arange(0, HEAD_DIM, 2, dtype=F32) / HEAD_DIM)
    ang = pos[:, None] * inv_freq[None, :]
    cos, sin = jnp.cos(ang), jnp.sin(ang)
    return jnp.concatenate([cos, cos], axis=-1), jnp.concatenate([-sin, sin], axis=-1)


def _ffn(h, w_gate, w_up, w_down):
    a = _gateup(h, w_gate, w_up)
    return _down(a, w_down.astype(BF16))


def kernel(x, ffn1_pre_g, ffn1_w_gate, ffn1_w_up, ffn1_w_down, ffn1_post_g, mix_pre_g, w_in, attn_sinks, pool_w, pool_scale, w_out, mix_post_g, ffn2_pre_g, ffn2_w_gate, ffn2_w_up, ffn2_w_down, ffn2_post_g):
    b, s, d = x.shape
    depth = ffn1_pre_g.shape[0]
    cos, sin_signed = _rope_tables(s)
    vec = lambda g: g.reshape(1, -1)
    xf = x.reshape(b * s, d)
    h = _prenorm(xf, vec(ffn1_pre_g[0]))
    for l in range(depth):
        y = _ffn(h, ffn1_w_gate[l], ffn1_w_up[l], ffn1_w_down[l])
        xf, h = _residual(xf, y, vec(ffn1_post_g[l]), vec(mix_pre_g[l]), FFN_RES_WEIGHT)

        proj = _in_proj(h, w_in[l])
        a = _attention(proj, attn_sinks[l], cos, sin_signed, b, s)
        pm = _pool(proj, pool_w[l], vec(pool_scale[l]), b, s)
        y = _out_proj(a, pm, w_out[l])
        xf, h = _residual(xf, y, vec(mix_post_g[l]), vec(ffn2_pre_g[l]), 1.0)

        y = _ffn(h, ffn2_w_gate[l], ffn2_w_up[l], ffn2_w_down[l])
        g_next = vec(ffn1_pre_g[l + 1]) if l + 1 < depth else None
        xf, h = _residual(xf, y, vec(ffn2_post_g[l]), g_next, FFN_RES_WEIGHT)
    return xf.reshape(b, s, d)
```

```python
import functools

import jax
import jax.numpy as jnp
from jax import lax
from jax.experimental import pallas as pl
from jax.experimental.pallas import tpu as pltpu

D_MODEL = 4096
HEAD_DIM = 128
N_HEADS = 16
N_KV_HEADS = 4
GQA_GROUP = N_HEADS // N_KV_HEADS
ATTN_WIDTH = N_HEADS * HEAD_DIM
KV_WIDTH = N_KV_HEADS * HEAD_DIM
BLOCK = 128
ROPE_THETA = 10000.0
POOL_WIDTH = D_MODEL - ATTN_WIDTH
POOL_WINDOWS = (2, 4, 8, 16)
POOL_GROUP_WIDTH = POOL_WIDTH // len(POOL_WINDOWS)
IN_PROJ_WIDTH = ATTN_WIDTH + 2 * KV_WIDTH + POOL_WIDTH
FFN_RES_WEIGHT = 0.5
RMS_EPS = 1e-6
POOL_HALO = 16

BF16 = jnp.bfloat16
F32 = jnp.float32
BF16_SUBLANES = 16

VMEM_LIMIT_BYTES = 56 * 1024 * 1024


def _params(n_axes):
    return pltpu.CompilerParams(
        dimension_semantics=("arbitrary",) * n_axes,
        vmem_limit_bytes=VMEM_LIMIT_BYTES,
    )


def _rms(x, g):
    ms = jnp.mean(x * x, axis=-1, keepdims=True)
    return x * lax.rsqrt(ms + RMS_EPS) * g


def _norm_kernel(x_ref, g_ref, h_ref):
    h_ref[...] = _rms(x_ref[...], g_ref[...]).astype(h_ref.dtype)


def _prenorm(x, g, tm=512):
    m, d = x.shape
    return pl.pallas_call(
        _norm_kernel,
        grid=(m // tm,),
        in_specs=[pl.BlockSpec((tm, d), lambda i: (i, 0)),
                  pl.BlockSpec((1, d), lambda i: (0, 0))],
        out_specs=pl.BlockSpec((tm, d), lambda i: (i, 0)),
        out_shape=jax.ShapeDtypeStruct((m, d), BF16),
        compiler_params=_params(1),
        name="prenorm",
    )(x, g)


def _residual_kernel(x_ref, y_ref, gpost_ref, *rest, weight, with_next):
    if with_next:
        gnext_ref, xo_ref, h_ref = rest
    else:
        (xo_ref,) = rest
    r = _rms(y_ref[...], gpost_ref[...])
    if weight != 1.0:
        r = weight * r
    xn = x_ref[...] + r
    xo_ref[...] = xn
    if with_next:
        h_ref[...] = _rms(xn, gnext_ref[...]).astype(h_ref.dtype)


def _residual(x, y, g_post, g_next, weight, tm=256):
    m, d = x.shape
    with_next = g_next is not None
    row = pl.BlockSpec((tm, d), lambda i: (i, 0))
    vec = pl.BlockSpec((1, d), lambda i: (0, 0))
    in_specs = [row, row, vec] + ([vec] if with_next else [])
    args = (x, y, g_post) + ((g_next,) if with_next else ())
    out_specs = [row] + ([row] if with_next else [])
    out_shape = [jax.ShapeDtypeStruct((m, d), F32)] + (
        [jax.ShapeDtypeStruct((m, d), BF16)] if with_next else [])
    outs = pl.pallas_call(
        functools.partial(_residual_kernel, weight=weight, with_next=with_next),
        grid=(m // tm,),
        in_specs=in_specs,
        out_specs=out_specs,
        out_shape=out_shape,
        compiler_params=_params(1),
        name="residual_norm",
    )(*args)
    return (outs[0], outs[1]) if with_next else (outs[0], None)


def _staged_kernel(*refs, n_x, n_w, n_extra, nj, slab, epilogue):
    x_refs = refs[:n_x]
    w_refs = refs[n_x:n_x + n_w]
    extra = refs[n_x + n_w:n_x + n_w + n_extra]
    o_ref = refs[n_x + n_w + n_extra]
    w_s = refs[n_x + n_w + n_extra + 1:]
    j = pl.program_id(0)
    i = pl.program_id(1)

    @pl.when(j < nj)
    def _():
        rows = pl.ds(pl.multiple_of(i * slab, slab), slab)
        for w_ref, s in zip(w_refs, w_s):
            s[j % 2, rows, :] = w_ref[...].astype(BF16)

    @pl.when(j > 0)
    def _():
        slot = (j + 1) % 2
        accs = []
        for s in w_s:
            acc, k0 = None, 0
            for x_ref in x_refs:
                kx = x_ref.shape[1]
                part = jnp.dot(x_ref[...], s[slot, k0:k0 + kx, :], preferred_element_type=F32)
                acc = part if acc is None else acc + part
                k0 += kx
            accs.append(acc)
        epilogue(accs, extra, o_ref)


def _staged_matmul(xs, ws, *, n, out_dtype, epilogue, name, tm, tn, col0=0,
                   extra=(), extra_specs=()):
    m = xs[0].shape[0]
    k = sum(x.shape[1] for x in xs)
    assert all(w.shape[0] == k for w in ws) and m % tm == 0 and col0 % tn == 0
    ni, nj = m // tm, pl.cdiv(n, tn)
    assert k % (ni * BF16_SUBLANES) == 0
    slab = k // ni
    cb = col0 // tn
    row_tile = lambda j, i: jnp.where(j == 0, 0, i)
    x_specs = [pl.BlockSpec((tm, x.shape[1]), lambda j, i: (row_tile(j, i), 0)) for x in xs]
    w_spec = pl.BlockSpec((slab, tn), lambda j, i: (jnp.where(j < nj, i, ni - 1),
                                                    cb + jnp.minimum(j, nj - 1)))
    o_spec = pl.BlockSpec((tm, tn), lambda j, i: (row_tile(j, i), jnp.maximum(j - 1, 0)))
    kern = functools.partial(_staged_kernel, n_x=len(xs), n_w=len(ws), n_extra=len(extra),
                             nj=nj, slab=slab, epilogue=epilogue)
    return pl.pallas_call(
        kern,
        grid=(nj + 1, ni),
        in_specs=x_specs + [w_spec] * len(ws) + list(extra_specs),
        out_specs=o_spec,
        out_shape=jax.ShapeDtypeStruct((m, n), out_dtype),
        scratch_shapes=[pltpu.VMEM((2, k, tn), BF16) for _ in ws],
        compiler_params=_params(2),
        name=name,
    )(*xs, *ws, *extra)


def _store_epilogue(accs, extra, o_ref):
    o_ref[...] = accs[0].astype(o_ref.dtype)


def _swiglu_epilogue(accs, extra, o_ref):
    g, u = accs
    o_ref[...] = (jax.nn.silu(g) * u).astype(o_ref.dtype)


def _rope(t, cos, sin_signed):
    return t * cos + pltpu.roll(t, HEAD_DIM // 2, 1) * sin_signed


def _rope_epilogue(accs, extra, o_ref):
    cos_ref, sin_ref = extra
    acc = accs[0]
    cos, sin = cos_ref[0], sin_ref[0]
    for c in range(acc.shape[1] // HEAD_DIM):
        cols = slice(c * HEAD_DIM, (c + 1) * HEAD_DIM)
        o_ref[:, cols] = _rope(acc[:, cols], cos, sin).astype(o_ref.dtype)


def _ffn(h, w_gate, w_up, w_down):
    a = _staged_matmul([h], [w_gate, w_up], n=w_gate.shape[1], out_dtype=BF16,
                       epilogue=_swiglu_epilogue, name="ffn_gate_up", tm=1024, tn=512)
    return _staged_matmul([a], [w_down], n=w_down.shape[1], out_dtype=F32,
                          epilogue=_store_epilogue, name="ffn_down", tm=512, tn=512)


def _in_proj_qkv(h, w, cos3, sin3, seq, tm=1024, tn=512):
    assert KV_WIDTH == tn and seq % tm == 0
    q_tiles = ATTN_WIDTH // tn
    table = lambda j, i: (jnp.maximum(j - q_tiles, 0), i % (seq // tm), 0)
    spec = pl.BlockSpec((1, tm, HEAD_DIM), table)
    return _staged_matmul([h], [w], n=ATTN_WIDTH + 2 * KV_WIDTH, out_dtype=BF16,
                          epilogue=_rope_epilogue, name="mix_in_proj_qkv", tm=tm, tn=tn,
                          extra=(cos3, sin3), extra_specs=(spec, spec))


def _in_proj_pool(h, w, tm=1024, tn=512):
    return _staged_matmul([h], [w], n=POOL_WIDTH, out_dtype=F32, epilogue=_store_epilogue,
                          name="mix_in_proj_pool", tm=tm, tn=tn, col0=ATTN_WIDTH + 2 * KV_WIDTH)


def _out_proj(a, pm, w, tm=1024, tn=512):
    return _staged_matmul([a, pm], [w], n=w.shape[1], out_dtype=F32, epilogue=_store_epilogue,
                          name="mix_out_proj", tm=tm, tn=tn)


def _attn_kernel(sinks_ref, q_ref, k_ref, v_ref, o_ref, bias_s, *, seq):
    kvh = pl.program_id(1)
    g, blk = GQA_GROUP, BLOCK
    key = lax.broadcasted_iota(jnp.int32, (2 * blk, blk), 0)
    qry = lax.broadcasted_iota(jnp.int32, (2 * blk, blk), 1)
    cur_ok = (key >= blk) & (key - blk <= qry)
    prev_ok = (key < blk) & (key > qry)
    bias_s[0] = jnp.where(cur_ok, 0.0, -jnp.inf)
    bias_s[1] = jnp.where(cur_ok | prev_ok, 0.0, -jnp.inf)
    sinks = [jnp.full((1, blk), sinks_ref[kvh * g + c], F32) for c in range(g)]
    contract_last = (((1,), (1,)), ((), ()))
    contract_first = (((0,), (0,)), ((), ()))

    def body(n, carry):
        r0 = pl.multiple_of(n * blk, blk)
        rp = pl.multiple_of(jnp.maximum(n - 1, 0) * blk, blk)
        kw = jnp.concatenate([k_ref[pl.ds(rp, blk), :], k_ref[pl.ds(r0, blk), :]], axis=0)
        vw = jnp.concatenate([v_ref[pl.ds(rp, blk), :], v_ref[pl.ds(r0, blk), :]], axis=0)
        q4 = jnp.concatenate(
            [q_ref[pl.ds(r0, blk), c * HEAD_DIM:(c + 1) * HEAD_DIM] for c in range(g)], axis=0)
        st = lax.dot_general(kw, q4, contract_last, preferred_element_type=F32)
        bias = bias_s[jnp.minimum(n, 1)]
        for c in range(g):
            s = st[:, c * blk:(c + 1) * blk] + bias
            m = jnp.maximum(jnp.max(s, axis=0, keepdims=True), sinks[c])
            e = jnp.exp(s - m)
            denom = jnp.sum(e, axis=0, keepdims=True) + jnp.exp(sinks[c] - m)
            p = (e * (1.0 / denom)).astype(BF16)
            o = lax.dot_general(p, vw, contract_first, preferred_element_type=F32)
            o_ref[pl.ds(r0, blk), c * HEAD_DIM:(c + 1) * HEAD_DIM] = o.astype(o_ref.dtype)
        return carry

    lax.fori_loop(0, seq // blk, body, 0, unroll=2)


def _attention(qkv, sinks, batch, seq):
    m = qkv.shape[0]
    qw = GQA_GROUP * HEAD_DIM
    k_blk0 = ATTN_WIDTH // HEAD_DIM
    v_blk0 = (ATTN_WIDTH + KV_WIDTH) // HEAD_DIM
    return pl.pallas_call(
        functools.partial(_attn_kernel, seq=seq),
        grid=(batch, N_KV_HEADS),
        in_specs=[pl.BlockSpec(memory_space=pltpu.SMEM),
                  pl.BlockSpec((seq, qw), lambda b, h: (b, h)),
                  pl.BlockSpec((seq, HEAD_DIM), lambda b, h: (b, k_blk0 + h)),
                  pl.BlockSpec((seq, HEAD_DIM), lambda b, h: (b, v_blk0 + h))],
        out_specs=pl.BlockSpec((seq, qw), lambda b, h: (b, h)),
        out_shape=jax.ShapeDtypeStruct((m, ATTN_WIDTH), BF16),
        scratch_shapes=[pltpu.VMEM((2, 2 * BLOCK, BLOCK), F32)],
        compiler_params=_params(2),
        name="swa_attention",
    )(sinks, qkv, qkv, qkv)


def _pool_kernel(p_ref, w_ref, sc_ref, o_ref, pad_s, y_s, *, seq, rows):
    grp = pl.program_id(1)
    gw = POOL_GROUP_WIDTH
    pad_s[0:POOL_HALO, :] = jnp.zeros((POOL_HALO, gw), F32)
    pad_s[POOL_HALO:, :] = p_ref[...]
    ext = rows + POOL_HALO
    lrow = lax.broadcasted_iota(jnp.int32, (rows, 128), 0)

    for gi, w in enumerate(POOL_WINDOWS):
        assert w & (w - 1) == 0 and w - 1 <= POOL_HALO

        @pl.when(grp == gi)
        def _(w=w):
            def body(r, carry):
                r0 = pl.multiple_of(r * rows, rows)
                cnt = jnp.minimum(lrow + (r0 + 1), w).astype(F32)
                for c in range(gw // 128):
                    x = pad_s[pl.ds(r0, ext), c * 128:(c + 1) * 128]
                    s = x
                    d = 1
                    while d < w:
                        s = s + pltpu.roll(s, d, 0)
                        d *= 2
                    y = s[POOL_HALO:] / cnt - x[POOL_HALO:]
                    y_s[pl.ds(r0, rows), c * 128:(c + 1) * 128] = y.astype(BF16)
                return carry

            lax.fori_loop(0, seq // rows, body, 0)

    out = jnp.dot(y_s[...], w_ref[0].astype(BF16), preferred_element_type=F32) * sc_ref[...]
    o_ref[...] = out.astype(o_ref.dtype)


def _pool(proj, pool_w, pool_scale, batch, seq, rows=128):
    m = proj.shape[0]
    gw = POOL_GROUP_WIDTH
    ng = len(POOL_WINDOWS)
    return pl.pallas_call(
        functools.partial(_pool_kernel, seq=seq, rows=rows),
        grid=(batch, ng),
        in_specs=[pl.BlockSpec((seq, gw), lambda b, g: (b, g)),
                  pl.BlockSpec((1, gw, gw), lambda b, g: (g, 0, 0)),
                  pl.BlockSpec((1, gw), lambda b, g: (0, g))],
        out_specs=pl.BlockSpec((seq, gw), lambda b, g: (b, g)),
        out_shape=jax.ShapeDtypeStruct((m, POOL_WIDTH), BF16),
        scratch_shapes=[pltpu.VMEM((seq + POOL_HALO, gw), F32), pltpu.VMEM((seq, gw), BF16)],
        compiler_params=_params(2),
        name="multiscale_pool",
    )(proj, pool_w, pool_scale)


def _rope_tables(seq):
    pos = jnp.arange(seq, dtype=F32)
    inv_freq = ROPE_THETA ** (-jnp.arange(0, HEAD_DIM, 2, dtype=F32) / HEAD_DIM)
    ang = pos[:, None] * inv_freq[None, :]
    cos, sin = jnp.cos(ang), jnp.sin(ang)
    cos = jnp.concatenate([cos, cos], axis=-1)
    sin = jnp.concatenate([-sin, sin], axis=-1)
    scale = HEAD_DIM ** -0.5
    cos3 = jnp.stack([cos * scale, cos, jnp.ones_like(cos)])
    sin3 = jnp.stack([sin * scale, sin, jnp.zeros_like(sin)])
    return cos3, sin3


def kernel(x, ffn1_pre_g, ffn1_w_gate, ffn1_w_up, ffn1_w_down, ffn1_post_g, mix_pre_g, w_in, attn_sinks, pool_w, pool_scale, w_out, mix_post_g, ffn2_pre_g, ffn2_w_gate, ffn2_w_up, ffn2_w_down, ffn2_post_g):
    b, s, d = x.shape
    depth = ffn1_pre_g.shape[0]
    cos3, sin3 = _rope_tables(s)
    vec = lambda g: g.reshape(1, -1)
    xf = x.reshape(b * s, d)
    h = _prenorm(xf, vec(ffn1_pre_g[0]))
    for l in range(depth):
        y = _ffn(h, ffn1_w_gate[l], ffn1_w_up[l], ffn1_w_down[l])
        xf, h = _residual(xf, y, vec(ffn1_post_g[l]), vec(mix_pre_g[l]), FFN_RES_WEIGHT)

        qkv = _in_proj_qkv(h, w_in[l], cos3, sin3, s)
        a = _attention(qkv, attn_sinks[l], b, s)
        pm = _pool(_in_proj_pool(h, w_in[l]), pool_w[l], vec(pool_scale[l]), b, s)
        y = _out_proj(a, pm, w_out[l])
        xf, h = _residual(xf, y, vec(mix_post_g[l]), vec(ffn2_pre_g[l]), 1.0)

        y = _ffn(h, ffn2_w_gate[l], ffn2_w_up[l], ffn2_w_down[l])
        g_next = vec(ffn1_pre_g[l + 1]) if l + 1 < depth else None
        xf, h = _residual(xf, y, vec(ffn2_post_g[l]), g_next, FFN_RES_WEIGHT)
    return xf.reshape(b, s, d)
```

```python
import functools

import jax
import jax.numpy as jnp
from jax import lax
from jax.experimental import pallas as pl
from jax.experimental.pallas import tpu as pltpu

D_MODEL = 4096
HEAD_DIM = 128
N_HEADS = 16
N_KV_HEADS = 4
GQA_GROUP = N_HEADS // N_KV_HEADS
ATTN_WIDTH = N_HEADS * HEAD_DIM
KV_WIDTH = N_KV_HEADS * HEAD_DIM
BLOCK = 128
ROPE_THETA = 10000.0
POOL_WIDTH = D_MODEL - ATTN_WIDTH
POOL_WINDOWS = (2, 4, 8, 16)
POOL_GROUP_WIDTH = POOL_WIDTH // len(POOL_WINDOWS)
IN_PROJ_WIDTH = ATTN_WIDTH + 2 * KV_WIDTH + POOL_WIDTH
FFN_RES_WEIGHT = 0.5
RMS_EPS = 1e-6
POOL_HALO = 16

BF16 = jnp.bfloat16
F32 = jnp.float32
BF16_SUBLANES = 16

VMEM_LIMIT_BYTES = 56 * 1024 * 1024


def _params(n_axes):
    return pltpu.CompilerParams(
        dimension_semantics=("arbitrary",) * n_axes,
        vmem_limit_bytes=VMEM_LIMIT_BYTES,
    )


def _rms(x, g):
    ms = jnp.mean(x * x, axis=-1, keepdims=True)
    return x * lax.rsqrt(ms + RMS_EPS) * g


def _norm_kernel(x_ref, g_ref, h_ref):
    h_ref[...] = _rms(x_ref[...], g_ref[...]).astype(h_ref.dtype)


def _prenorm(x, g, tm=512):
    m, d = x.shape
    return pl.pallas_call(
        _norm_kernel,
        grid=(m // tm,),
        in_specs=[pl.BlockSpec((tm, d), lambda i: (i, 0)),
                  pl.BlockSpec((1, d), lambda i: (0, 0))],
        out_specs=pl.BlockSpec((tm, d), lambda i: (i, 0)),
        out_shape=jax.ShapeDtypeStruct((m, d), BF16),
        compiler_params=_params(1),
        name="prenorm",
    )(x, g)


def _residual_kernel(x_ref, y_ref, gpost_ref, *rest, weight, with_next):
    if with_next:
        gnext_ref, xo_ref, h_ref = rest
    else:
        (xo_ref,) = rest
    r = _rms(y_ref[...], gpost_ref[...])
    if weight != 1.0:
        r = weight * r
    xn = x_ref[...] + r
    xo_ref[...] = xn
    if with_next:
        h_ref[...] = _rms(xn, gnext_ref[...]).astype(h_ref.dtype)


def _residual(x, y, g_post, g_next, weight, tm=256):
    m, d = x.shape
    with_next = g_next is not None
    row = pl.BlockSpec((tm, d), lambda i: (i, 0))
    vec = pl.BlockSpec((1, d), lambda i: (0, 0))
    in_specs = [row, row, vec] + ([vec] if with_next else [])
    args = (x, y, g_post) + ((g_next,) if with_next else ())
    out_specs = [row] + ([row] if with_next else [])
    out_shape = [jax.ShapeDtypeStruct((m, d), F32)] + (
        [jax.ShapeDtypeStruct((m, d), BF16)] if with_next else [])
    outs = pl.pallas_call(
        functools.partial(_residual_kernel, weight=weight, with_next=with_next),
        grid=(m // tm,),
        in_specs=in_specs,
        out_specs=out_specs,
        out_shape=out_shape,
        compiler_params=_params(1),
        name="residual_norm",
    )(*args)
    return (outs[0], outs[1]) if with_next else (outs[0], None)


def _staged_kernel(*refs, n_x, n_w, n_extra, nj, slab, epilogue):
    x_refs = refs[:n_x]
    w_refs = refs[n_x:n_x + n_w]
    extra = refs[n_x + n_w:n_x + n_w + n_extra]
    o_ref = refs[n_x + n_w + n_extra]
    w_s = refs[n_x + n_w + n_extra + 1:]
    j = pl.program_id(0)
    i = pl.program_id(1)

    @pl.when(j < nj)
    def _():
        rows = pl.ds(pl.multiple_of(i * slab, slab), slab)
        for w_ref, s in zip(w_refs, w_s):
            s[j % 2, rows, :] = w_ref[...].astype(BF16)

    @pl.when(j > 0)
    def _():
        slot = (j + 1) % 2
        accs = []
        for s in w_s:
            acc, k0 = None, 0
            for x_ref in x_refs:
                kx = x_ref.shape[1]
                part = jnp.dot(x_ref[...], s[slot, k0:k0 + kx, :], preferred_element_type=F32)
                acc = part if acc is None else acc + part
                k0 += kx
            accs.append(acc)
        epilogue(accs, extra, o_ref)


def _staged_matmul(xs, ws, *, n, out_dtype, epilogue, name, tm, tn, col0=0,
                   extra=(), extra_specs=()):
    m = xs[0].shape[0]
    k = sum(x.shape[1] for x in xs)
    assert all(w.shape[0] == k for w in ws) and m % tm == 0 and col0 % tn == 0
    ni, nj = m // tm, pl.cdiv(n, tn)
    assert k % (ni * BF16_SUBLANES) == 0
    slab = k // ni
    cb = col0 // tn
    row_tile = lambda j, i: jnp.where(j == 0, 0, i)
    x_specs = [pl.BlockSpec((tm, x.shape[1]), lambda j, i: (row_tile(j, i), 0)) for x in xs]
    w_spec = pl.BlockSpec((slab, tn), lambda j, i: (jnp.where(j < nj, i, ni - 1),
                                                    cb + jnp.minimum(j, nj - 1)))
    o_spec = pl.BlockSpec((tm, tn), lambda j, i: (row_tile(j, i), jnp.maximum(j - 1, 0)))
    kern = functools.partial(_staged_kernel, n_x=len(xs), n_w=len(ws), n_extra=len(extra),
                             nj=nj, slab=slab, epilogue=epilogue)
    return pl.pallas_call(
        kern,
        grid=(nj + 1, ni),
        in_specs=x_specs + [w_spec] * len(ws) + list(extra_specs),
        out_specs=o_spec,
        out_shape=jax.ShapeDtypeStruct((m, n), out_dtype),
        scratch_shapes=[pltpu.VMEM((2, k, tn), BF16) for _ in ws],
        compiler_params=_params(2),
        name=name,
    )(*xs, *ws, *extra)


def _store_epilogue(accs, extra, o_ref):
    o_ref[...] = accs[0].astype(o_ref.dtype)


def _swiglu_epilogue(accs, extra, o_ref):
    g, u = accs
    o_ref[...] = (jax.nn.silu(g) * u).astype(o_ref.dtype)


def _rope(t, cos, sin_signed):
    return t * cos + pltpu.roll(t, HEAD_DIM // 2, 1) * sin_signed


def _ffn(h, w_gate, w_up, w_down):
    a = _staged_matmul([h], [w_gate, w_up], n=w_gate.shape[1], out_dtype=BF16,
                       epilogue=_swiglu_epilogue, name="ffn_gate_up", tm=1024, tn=512)
    return _staged_matmul([a], [w_down], n=w_down.shape[1], out_dtype=F32,
                          epilogue=_store_epilogue, name="ffn_down", tm=512, tn=512)


def _in_proj_pool(h, w, tm=1024, tn=512):
    return _staged_matmul([h], [w], n=POOL_WIDTH, out_dtype=F32, epilogue=_store_epilogue,
                          name="mix_in_proj_pool", tm=tm, tn=tn, col0=ATTN_WIDTH + 2 * KV_WIDTH)


def _stage_resident(t, ns, slab, w_ref, w_s):
    @pl.when(t < ns)
    def _():
        w_s[pl.ds(pl.multiple_of(t * slab, slab), slab), :] = w_ref[...].astype(BF16)


def _residual_qkv_kernel(x_ref, y_ref, gpost_ref, gnext_ref, w_ref, cos_ref, sin_ref,
                         xo_ref, h_ref, qkv_ref, w_s, *, ns, slab, weight):
    t = pl.program_id(0)
    _stage_resident(t, ns, slab, w_ref, w_s)

    @pl.when(t >= ns)
    def _():
        xn = x_ref[...] + weight * _rms(y_ref[...], gpost_ref[...])
        xo_ref[...] = xn
        h = _rms(xn, gnext_ref[...]).astype(BF16)
        h_ref[...] = h
        acc = jnp.dot(h, w_s[...], preferred_element_type=F32)
        for c in range(acc.shape[1] // HEAD_DIM):
            cols = slice(c * HEAD_DIM, (c + 1) * HEAD_DIM)
            if c * HEAD_DIM < ATTN_WIDTH + KV_WIDTH:
                kind = 0 if c * HEAD_DIM < ATTN_WIDTH else 1
                val = _rope(acc[:, cols], cos_ref[kind], sin_ref[kind])
            else:
                val = acc[:, cols]
            qkv_ref[:, cols] = val.astype(qkv_ref.dtype)


def _residual_qkv(x, y, g_post, g_next, w_in, cos2, sin2, seq, weight, tm=128, ns=32):
    m, d = x.shape
    n = ATTN_WIDTH + 2 * KV_WIDTH
    assert m % tm == 0 and seq % tm == 0 and d % (ns * BF16_SUBLANES) == 0
    slab = d // ns
    tile = lambda t: jnp.maximum(t - ns, 0)
    row = lambda width: pl.BlockSpec((tm, width), lambda t: (tile(t), 0))
    vec = pl.BlockSpec((1, d), lambda t: (0, 0))
    table = pl.BlockSpec((2, tm, HEAD_DIM), lambda t: (0, tile(t) % (seq // tm), 0))
    return pl.pallas_call(
        functools.partial(_residual_qkv_kernel, ns=ns, slab=slab, weight=weight),
        grid=(ns + m // tm,),
        in_specs=[row(d), row(d), vec, vec,
                  pl.BlockSpec((slab, n), lambda t: (jnp.minimum(t, ns - 1), 0)),
                  table, table],
        out_specs=[row(d), row(d), row(n)],
        out_shape=[jax.ShapeDtypeStruct((m, d), F32), jax.ShapeDtypeStruct((m, d), BF16),
                   jax.ShapeDtypeStruct((m, n), BF16)],
        scratch_shapes=[pltpu.VMEM((d, n), BF16)],
        compiler_params=_params(1),
        name="residual_qkv",
    )(x, y, g_post, g_next, w_in, cos2, sin2)


def _out_proj_residual_kernel(a_ref, p_ref, x_ref, gpost_ref, gnext_ref, w_ref,
                              xo_ref, h_ref, w_s, *, ns, slab):
    t = pl.program_id(0)
    _stage_resident(t, ns, slab, w_ref, w_s)

    @pl.when(t >= ns)
    def _():
        ka = a_ref.shape[1]
        y = (jnp.dot(a_ref[...], w_s[0:ka, :], preferred_element_type=F32)
             + jnp.dot(p_ref[...], w_s[ka:, :], preferred_element_type=F32))
        xn = x_ref[...] + _rms(y, gpost_ref[...])
        xo_ref[...] = xn
        h_ref[...] = _rms(xn, gnext_ref[...]).astype(h_ref.dtype)


def _out_proj_residual(a, pm, x, g_post, g_next, w, tm=128, ns=32):
    m, d = x.shape
    k = a.shape[1] + pm.shape[1]
    assert w.shape == (k, d) and m % tm == 0 and k % (ns * BF16_SUBLANES) == 0
    slab = k // ns
    row = lambda width: pl.BlockSpec((tm, width), lambda t: (jnp.maximum(t - ns, 0), 0))
    vec = pl.BlockSpec((1, d), lambda t: (0, 0))
    return pl.pallas_call(
        functools.partial(_out_proj_residual_kernel, ns=ns, slab=slab),
        grid=(ns + m // tm,),
        in_specs=[row(a.shape[1]), row(pm.shape[1]), row(d), vec, vec,
                  pl.BlockSpec((slab, d), lambda t: (jnp.minimum(t, ns - 1), 0))],
        out_specs=[row(d), row(d)],
        out_shape=[jax.ShapeDtypeStruct((m, d), F32), jax.ShapeDtypeStruct((m, d), BF16)],
        scratch_shapes=[pltpu.VMEM((k, d), BF16)],
        compiler_params=_params(1),
        name="out_proj_residual",
    )(a, pm, x, g_post, g_next, w)


def _attn_kernel(sinks_ref, q_ref, k_ref, v_ref, o_ref, bias_s, *, seq):
    kvh = pl.program_id(1)
    g, blk = GQA_GROUP, BLOCK
    key = lax.broadcasted_iota(jnp.int32, (2 * blk, blk), 0)
    qry = lax.broadcasted_iota(jnp.int32, (2 * blk, blk), 1)
    cur_ok = (key >= blk) & (key - blk <= qry)
    prev_ok = (key < blk) & (key > qry)
    bias_s[0] = jnp.where(cur_ok, 0.0, -jnp.inf)
    bias_s[1] = jnp.where(cur_ok | prev_ok, 0.0, -jnp.inf)
    sinks = [jnp.full((1, blk), sinks_ref[kvh * g + c], F32) for c in range(g)]
    contract_last = (((1,), (1,)), ((), ()))
    contract_first = (((0,), (0,)), ((), ()))

    def body(n, carry):
        r0 = pl.multiple_of(n * blk, blk)
        rp = pl.multiple_of(jnp.maximum(n - 1, 0) * blk, blk)
        kw = jnp.concatenate([k_ref[pl.ds(rp, blk), :], k_ref[pl.ds(r0, blk), :]], axis=0)
        vw = jnp.concatenate([v_ref[pl.ds(rp, blk), :], v_ref[pl.ds(r0, blk), :]], axis=0)
        q4 = jnp.concatenate(
            [q_ref[pl.ds(r0, blk), c * HEAD_DIM:(c + 1) * HEAD_DIM] for c in range(g)], axis=0)
        st = lax.dot_general(kw, q4, contract_last, preferred_element_type=F32)
        bias = bias_s[jnp.minimum(n, 1)]
        for c in range(g):
            s = st[:, c * blk:(c + 1) * blk] + bias
            m = jnp.maximum(jnp.max(s, axis=0, keepdims=True), sinks[c])
            e = jnp.exp(s - m)
            denom = jnp.sum(e, axis=0, keepdims=True) + jnp.exp(sinks[c] - m)
            p = (e * (1.0 / denom)).astype(BF16)
            o = lax.dot_general(p, vw, contract_first, preferred_element_type=F32)
            o_ref[pl.ds(r0, blk), c * HEAD_DIM:(c + 1) * HEAD_DIM] = o.astype(o_ref.dtype)
        return carry

    lax.fori_loop(0, seq // blk, body, 0, unroll=4)


def _attention(qkv, sinks, batch, seq):
    m = qkv.shape[0]
    qw = GQA_GROUP * HEAD_DIM
    k_blk0 = ATTN_WIDTH // HEAD_DIM
    v_blk0 = (ATTN_WIDTH + KV_WIDTH) // HEAD_DIM
    return pl.pallas_call(
        functools.partial(_attn_kernel, seq=seq),
        grid=(batch, N_KV_HEADS),
        in_specs=[pl.BlockSpec(memory_space=pltpu.SMEM),
                  pl.BlockSpec((seq, qw), lambda b, h: (b, h)),
                  pl.BlockSpec((seq, HEAD_DIM), lambda b, h: (b, k_blk0 + h)),
                  pl.BlockSpec((seq, HEAD_DIM), lambda b, h: (b, v_blk0 + h))],
        out_specs=pl.BlockSpec((seq, qw), lambda b, h: (b, h)),
        out_shape=jax.ShapeDtypeStruct((m, ATTN_WIDTH), BF16),
        scratch_shapes=[pltpu.VMEM((2, 2 * BLOCK, BLOCK), F32)],
        compiler_params=_params(2),
        name="swa_attention",
    )(sinks, qkv, qkv, qkv)


def _pool_kernel(p_ref, w_ref, sc_ref, o_ref, pad_s, y_s, *, seq, rows):
    grp = pl.program_id(1)
    gw = POOL_GROUP_WIDTH
    pad_s[0:POOL_HALO, :] = jnp.zeros((POOL_HALO, gw), F32)
    pad_s[POOL_HALO:, :] = p_ref[...]
    ext = rows + POOL_HALO
    lrow = lax.broadcasted_iota(jnp.int32, (rows, 128), 0)

    for gi, w in enumerate(POOL_WINDOWS):
        assert w & (w - 1) == 0 and w - 1 <= POOL_HALO

        @pl.when(grp == gi)
        def _(w=w):
            def body(r, carry):
                r0 = pl.multiple_of(r * rows, rows)
                cnt = jnp.minimum(lrow + (r0 + 1), w).astype(F32)
                for c in range(gw // 128):
                    x = pad_s[pl.ds(r0, ext), c * 128:(c + 1) * 128]
                    s = x
                    d = 1
                    while d < w:
                        s = s + pltpu.roll(s, d, 0)
                        d *= 2
                    y = s[POOL_HALO:] / cnt - x[POOL_HALO:]
                    y_s[pl.ds(r0, rows), c * 128:(c + 1) * 128] = y.astype(BF16)
                return carry

            lax.fori_loop(0, seq // rows, body, 0)

    out = jnp.dot(y_s[...], w_ref[0].astype(BF16), preferred_element_type=F32) * sc_ref[...]
    o_ref[...] = out.astype(o_ref.dtype)


def _pool(proj, pool_w, pool_scale, batch, seq, rows=128):
    m = proj.shape[0]
    gw = POOL_GROUP_WIDTH
    ng = len(POOL_WINDOWS)
    return pl.pallas_call(
        functools.partial(_pool_kernel, seq=seq, rows=rows),
        grid=(batch, ng),
        in_specs=[pl.BlockSpec((seq, gw), lambda b, g: (b, g)),
                  pl.BlockSpec((1, gw, gw), lambda b, g: (g, 0, 0)),
                  pl.BlockSpec((1, gw), lambda b, g: (0, g))],
        out_specs=pl.BlockSpec((seq, gw), lambda b, g: (b, g)),
        out_shape=jax.ShapeDtypeStruct((m, POOL_WIDTH), BF16),
        scratch_shapes=[pltpu.VMEM((seq + POOL_HALO, gw), F32), pltpu.VMEM((seq, gw), BF16)],
        compiler_params=_params(2),
        name="multiscale_pool",
    )(proj, pool_w, pool_scale)


def _rope_tables(seq):
    pos = jnp.arange(seq, dtype=F32)
    inv_freq = ROPE_THETA ** (-jnp.arange(0, HEAD_DIM, 2, dtype=F32) / HEAD_DIM)
    ang = pos[:, None] * inv_freq[None, :]
    cos, sin = jnp.cos(ang), jnp.sin(ang)
    cos = jnp.concatenate([cos, cos], axis=-1)
    sin = jnp.concatenate([-sin, sin], axis=-1)
    scale = HEAD_DIM ** -0.5
    return jnp.stack([cos * scale, cos]), jnp.stack([sin * scale, sin])


def kernel(x, ffn1_pre_g, ffn1_w_gate, ffn1_w_up, ffn1_w_down, ffn1_post_g, mix_pre_g, w_in, attn_sinks, pool_w, pool_scale, w_out, mix_post_g, ffn2_pre_g, ffn2_w_gate, ffn2_w_up, ffn2_w_down, ffn2_post_g):
    b, s, d = x.shape
    depth = ffn1_pre_g.shape[0]
    cos2, sin2 = _rope_tables(s)
    vec = lambda g: g.reshape(1, -1)
    xf = x.reshape(b * s, d)
    h = _prenorm(xf, vec(ffn1_pre_g[0]))
    for l in range(depth):
        y = _ffn(h, ffn1_w_gate[l], ffn1_w_up[l], ffn1_w_down[l])
        xf, h, qkv = _residual_qkv(xf, y, vec(ffn1_post_g[l]), vec(mix_pre_g[l]), w_in[l],
                                   cos2, sin2, s, FFN_RES_WEIGHT)
        a = _attention(qkv, attn_sinks[l], b, s)
        pm = _pool(_in_proj_pool(h, w_in[l]), pool_w[l], vec(pool_scale[l]), b, s)
        xf, h = _out_proj_residual(a, pm, xf, vec(mix_post_g[l]), vec(ffn2_pre_g[l]), w_out[l])

        y = _ffn(h, ffn2_w_gate[l], ffn2_w_up[l], ffn2_w_down[l])
        g_next = vec(ffn1_pre_g[l + 1]) if l + 1 < depth else None
        xf, h = _residual(xf, y, vec(ffn2_post_g[l]), g_next, FFN_RES_WEIGHT)
    return xf.reshape(b, s, d)
```

```python
import functools

import jax
import jax.numpy as jnp
from jax import lax
from jax.experimental import pallas as pl
from jax.experimental.pallas import tpu as pltpu

D_MODEL = 4096
HEAD_DIM = 128
N_HEADS = 16
N_KV_HEADS = 4
GQA_GROUP = N_HEADS // N_KV_HEADS
ATTN_WIDTH = N_HEADS * HEAD_DIM
KV_WIDTH = N_KV_HEADS * HEAD_DIM
BLOCK = 128
ROPE_THETA = 10000.0
POOL_WIDTH = D_MODEL - ATTN_WIDTH
POOL_WINDOWS = (2, 4, 8, 16)
POOL_GROUP_WIDTH = POOL_WIDTH // len(POOL_WINDOWS)
IN_PROJ_WIDTH = ATTN_WIDTH + 2 * KV_WIDTH + POOL_WIDTH
FFN_RES_WEIGHT = 0.5
RMS_EPS = 1e-6
POOL_HALO = 16

BF16 = jnp.bfloat16
F32 = jnp.float32
BF16_SUBLANES = 16

VMEM_LIMIT_BYTES = 56 * 1024 * 1024


def _params(n_axes):
    return pltpu.CompilerParams(
        dimension_semantics=("arbitrary",) * n_axes,
        vmem_limit_bytes=VMEM_LIMIT_BYTES,
    )


def _rms(x, g):
    ms = jnp.mean(x * x, axis=-1, keepdims=True)
    return x * lax.rsqrt(ms + RMS_EPS) * g


def _norm_kernel(x_ref, g_ref, h_ref):
    h_ref[...] = _rms(x_ref[...], g_ref[...]).astype(h_ref.dtype)


def _prenorm(x, g, tm=512):
    m, d = x.shape
    return pl.pallas_call(
        _norm_kernel,
        grid=(m // tm,),
        in_specs=[pl.BlockSpec((tm, d), lambda i: (i, 0)),
                  pl.BlockSpec((1, d), lambda i: (0, 0))],
        out_specs=pl.BlockSpec((tm, d), lambda i: (i, 0)),
        out_shape=jax.ShapeDtypeStruct((m, d), BF16),
        compiler_params=_params(1),
        name="prenorm",
    )(x, g)


def _residual_kernel(x_ref, y_ref, gpost_ref, *rest, weight, with_next):
    if with_next:
        gnext_ref, xo_ref, h_ref = rest
    else:
        (xo_ref,) = rest
    r = _rms(y_ref[...], gpost_ref[...])
    if weight != 1.0:
        r = weight * r
    xn = x_ref[...] + r
    xo_ref[...] = xn
    if with_next:
        h_ref[...] = _rms(xn, gnext_ref[...]).astype(h_ref.dtype)


def _residual(x, y, g_post, g_next, weight, tm=256):
    m, d = x.shape
    with_next = g_next is not None
    row = pl.BlockSpec((tm, d), lambda i: (i, 0))
    vec = pl.BlockSpec((1, d), lambda i: (0, 0))
    in_specs = [row, row, vec] + ([vec] if with_next else [])
    args = (x, y, g_post) + ((g_next,) if with_next else ())
    out_specs = [row] + ([row] if with_next else [])
    out_shape = [jax.ShapeDtypeStruct((m, d), F32)] + (
        [jax.ShapeDtypeStruct((m, d), BF16)] if with_next else [])
    outs = pl.pallas_call(
        functools.partial(_residual_kernel, weight=weight, with_next=with_next),
        grid=(m // tm,),
        in_specs=in_specs,
        out_specs=out_specs,
        out_shape=out_shape,
        compiler_params=_params(1),
        name="residual_norm",
    )(*args)
    return (outs[0], outs[1]) if with_next else (outs[0], None)


def _staged_kernel(*refs, n_x, n_w, n_extra, nj, slab, epilogue):
    x_refs = refs[:n_x]
    w_refs = refs[n_x:n_x + n_w]
    extra = refs[n_x + n_w:n_x + n_w + n_extra]
    o_ref = refs[n_x + n_w + n_extra]
    w_s = refs[n_x + n_w + n_extra + 1:]
    j = pl.program_id(0)
    i = pl.program_id(1)

    def stage():
        rows = pl.ds(pl.multiple_of(i * slab, slab), slab)
        for w_ref, s in zip(w_refs, w_s):
            s[j % 2, rows, :] = w_ref[...].astype(BF16)

    pl.when(j == 0)(stage)

    @pl.when(j > 0)
    def _():
        stage()
        slot = (j + 1) % 2
        accs = []
        for s in w_s:
            acc, k0 = None, 0
            for x_ref in x_refs:
                kx = x_ref.shape[1]
                part = jnp.dot(x_ref[...], s[slot, k0:k0 + kx, :], preferred_element_type=F32)
                acc = part if acc is None else acc + part
                k0 += kx
            accs.append(acc)
        epilogue(accs, extra, o_ref)


def _staged_matmul(xs, ws, *, n, out_dtype, epilogue, name, tm, tn, col0=0,
                   extra=(), extra_specs=()):
    m = xs[0].shape[0]
    k = sum(x.shape[1] for x in xs)
    assert all(w.shape[0] == k for w in ws) and m % tm == 0 and col0 % tn == 0
    ni, nj = m // tm, pl.cdiv(n, tn)
    assert k % (ni * BF16_SUBLANES) == 0
    slab = k // ni
    cb = col0 // tn
    row_tile = lambda j, i: jnp.where(j == 0, 0, i)
    x_specs = [pl.BlockSpec((tm, x.shape[1]), lambda j, i: (row_tile(j, i), 0)) for x in xs]
    w_spec = pl.BlockSpec((slab, tn), lambda j, i: (jnp.where(j < nj, i, ni - 1),
                                                    cb + jnp.minimum(j, nj - 1)))
    o_spec = pl.BlockSpec((tm, tn), lambda j, i: (row_tile(j, i), jnp.maximum(j - 1, 0)))
    kern = functools.partial(_staged_kernel, n_x=len(xs), n_w=len(ws), n_extra=len(extra),
                             nj=nj, slab=slab, epilogue=epilogue)
    return pl.pallas_call(
        kern,
        grid=(nj + 1, ni),
        in_specs=x_specs + [w_spec] * len(ws) + list(extra_specs),
        out_specs=o_spec,
        out_shape=jax.ShapeDtypeStruct((m, n), out_dtype),
        scratch_shapes=[pltpu.VMEM((2, k, tn), BF16) for _ in ws],
        compiler_params=_params(2),
        name=name,
    )(*xs, *ws, *extra)


def _store_epilogue(accs, extra, o_ref):
    o_ref[...] = accs[0].astype(o_ref.dtype)


def _swiglu_epilogue(accs, extra, o_ref):
    g, u = accs
    o_ref[...] = (jax.nn.silu(g) * u).astype(o_ref.dtype)


def _rope(t, cos, sin_signed):
    return t * cos + pltpu.roll(t, HEAD_DIM // 2, 1) * sin_signed


def _ffn(h, w_gate, w_up, w_down):
    a = _staged_matmul([h], [w_gate, w_up], n=w_gate.shape[1], out_dtype=BF16,
                       epilogue=_swiglu_epilogue, name="ffn_gate_up", tm=1024, tn=512)
    return _staged_matmul([a], [w_down], n=w_down.shape[1], out_dtype=F32,
                          epilogue=_store_epilogue, name="ffn_down", tm=512, tn=512)


def _in_proj_pool(h, w, tm=1024, tn=512):
    return _staged_matmul([h], [w], n=POOL_WIDTH, out_dtype=F32, epilogue=_store_epilogue,
                          name="mix_in_proj_pool", tm=tm, tn=tn, col0=ATTN_WIDTH + 2 * KV_WIDTH)


def _stage_resident(t, ns, slab, w_ref, w_s):
    @pl.when(t < ns)
    def _():
        w_s[pl.ds(pl.multiple_of(t * slab, slab), slab), :] = w_ref[...].astype(BF16)


def _residual_qkv_kernel(x_ref, y_ref, gpost_ref, gnext_ref, w_ref, cos_ref, sin_ref,
                         xo_ref, h_ref, qkv_ref, w_s, *, ns, slab, weight):
    t = pl.program_id(0)
    _stage_resident(t, ns, slab, w_ref, w_s)

    @pl.when(t >= ns)
    def _():
        xn = x_ref[...] + weight * _rms(y_ref[...], gpost_ref[...])
        xo_ref[...] = xn
        h = _rms(xn, gnext_ref[...]).astype(BF16)
        h_ref[...] = h
        acc = jnp.dot(h, w_s[...], preferred_element_type=F32)
        for c in range(acc.shape[1] // HEAD_DIM):
            cols = slice(c * HEAD_DIM, (c + 1) * HEAD_DIM)
            if c * HEAD_DIM < ATTN_WIDTH + KV_WIDTH:
                kind = 0 if c * HEAD_DIM < ATTN_WIDTH else 1
                val = _rope(acc[:, cols], cos_ref[kind], sin_ref[kind])
            else:
                val = acc[:, cols]
            qkv_ref[:, cols] = val.astype(qkv_ref.dtype)


def _residual_qkv(x, y, g_post, g_next, w_in, cos2, sin2, seq, weight, tm=128, ns=32):
    m, d = x.shape
    n = ATTN_WIDTH + 2 * KV_WIDTH
    assert m % tm == 0 and seq % tm == 0 and d % (ns * BF16_SUBLANES) == 0
    slab = d // ns
    tile = lambda t: jnp.maximum(t - ns, 0)
    row = lambda width: pl.BlockSpec((tm, width), lambda t: (tile(t), 0))
    vec = pl.BlockSpec((1, d), lambda t: (0, 0))
    table = pl.BlockSpec((2, tm, HEAD_DIM), lambda t: (0, tile(t) % (seq // tm), 0))
    return pl.pallas_call(
        functools.partial(_residual_qkv_kernel, ns=ns, slab=slab, weight=weight),
        grid=(ns + m // tm,),
        in_specs=[row(d), row(d), vec, vec,
                  pl.BlockSpec((slab, n), lambda t: (jnp.minimum(t, ns - 1), 0)),
                  table, table],
        out_specs=[row(d), row(d), row(n)],
        out_shape=[jax.ShapeDtypeStruct((m, d), F32), jax.ShapeDtypeStruct((m, d), BF16),
                   jax.ShapeDtypeStruct((m, n), BF16)],
        scratch_shapes=[pltpu.VMEM((d, n), BF16)],
        compiler_params=_params(1),
        name="residual_qkv",
    )(x, y, g_post, g_next, w_in, cos2, sin2)


def _out_proj_residual_kernel(a_ref, p_ref, x_ref, gpost_ref, gnext_ref, w_ref,
                              xo_ref, h_ref, w_s, *, ns, slab):
    t = pl.program_id(0)
    _stage_resident(t, ns, slab, w_ref, w_s)

    @pl.when(t >= ns)
    def _():
        ka = a_ref.shape[1]
        y = (jnp.dot(a_ref[...], w_s[0:ka, :], preferred_element_type=F32)
             + jnp.dot(p_ref[...], w_s[ka:, :], preferred_element_type=F32))
        xn = x_ref[...] + _rms(y, gpost_ref[...])
        xo_ref[...] = xn
        h_ref[...] = _rms(xn, gnext_ref[...]).astype(h_ref.dtype)


def _out_proj_residual(a, pm, x, g_post, g_next, w, tm=128, ns=32):
    m, d = x.shape
    k = a.shape[1] + pm.shape[1]
    assert w.shape == (k, d) and m % tm == 0 and k % (ns * BF16_SUBLANES) == 0
    slab = k // ns
    row = lambda width: pl.BlockSpec((tm, width), lambda t: (jnp.maximum(t - ns, 0), 0))
    vec = pl.BlockSpec((1, d), lambda t: (0, 0))
    return pl.pallas_call(
        functools.partial(_out_proj_residual_kernel, ns=ns, slab=slab),
        grid=(ns + m // tm,),
        in_specs=[row(a.shape[1]), row(pm.shape[1]), row(d), vec, vec,
                  pl.BlockSpec((slab, d), lambda t: (jnp.minimum(t, ns - 1), 0))],
        out_specs=[row(d), row(d)],
        out_shape=[jax.ShapeDtypeStruct((m, d), F32), jax.ShapeDtypeStruct((m, d), BF16)],
        scratch_shapes=[pltpu.VMEM((k, d), BF16)],
        compiler_params=_params(1),
        name="out_proj_residual",
    )(a, pm, x, g_post, g_next, w)


def _attn_kernel(sinks_ref, q_ref, k_ref, v_ref, o_ref, bias_s, *, seq):
    kvh = pl.program_id(1)
    g, blk = GQA_GROUP, BLOCK
    key = lax.broadcasted_iota(jnp.int32, (2 * blk, blk), 0)
    qry = lax.broadcasted_iota(jnp.int32, (2 * blk, blk), 1)
    cur_ok = (key >= blk) & (key - blk <= qry)
    prev_ok = (key < blk) & (key > qry)
    bias_s[0] = jnp.where(cur_ok, 0.0, -jnp.inf)
    bias_s[1] = jnp.where(cur_ok | prev_ok, 0.0, -jnp.inf)
    sinks = [jnp.full((1, blk), sinks_ref[kvh * g + c], F32) for c in range(g)]
    contract_last = (((1,), (1,)), ((), ()))
    contract_first = (((0,), (0,)), ((), ()))

    def body(n, carry):
        r0 = pl.multiple_of(n * blk, blk)
        rp = pl.multiple_of(jnp.maximum(n - 1, 0) * blk, blk)
        kw = jnp.concatenate([k_ref[pl.ds(rp, blk), :], k_ref[pl.ds(r0, blk), :]], axis=0)
        vw = jnp.concatenate([v_ref[pl.ds(rp, blk), :], v_ref[pl.ds(r0, blk), :]], axis=0)
        q4 = jnp.concatenate(
            [q_ref[pl.ds(r0, blk), c * HEAD_DIM:(c + 1) * HEAD_DIM] for c in range(g)], axis=0)
        st = lax.dot_general(kw, q4, contract_last, preferred_element_type=F32)
        bias = bias_s[jnp.minimum(n, 1)]
        for c in range(g):
            s = st[:, c * blk:(c + 1) * blk] + bias
            m = jnp.maximum(jnp.max(s, axis=0, keepdims=True), sinks[c])
            e = jnp.exp(s - m)
            denom = jnp.sum(e, axis=0, keepdims=True) + jnp.exp(sinks[c] - m)
            p = (e * (1.0 / denom)).astype(BF16)
            o = lax.dot_general(p, vw, contract_first, preferred_element_type=F32)
            o_ref[pl.ds(r0, blk), c * HEAD_DIM:(c + 1) * HEAD_DIM] = o.astype(o_ref.dtype)
        return carry

    lax.fori_loop(0, seq // blk, body, 0, unroll=4)


def _attention(qkv, sinks, batch, seq):
    m = qkv.shape[0]
    qw = GQA_GROUP * HEAD_DIM
    k_blk0 = ATTN_WIDTH // HEAD_DIM
    v_blk0 = (ATTN_WIDTH + KV_WIDTH) // HEAD_DIM
    return pl.pallas_call(
        functools.partial(_attn_kernel, seq=seq),
        grid=(batch, N_KV_HEADS),
        in_specs=[pl.BlockSpec(memory_space=pltpu.SMEM),
                  pl.BlockSpec((seq, qw), lambda b, h: (b, h)),
                  pl.BlockSpec((seq, HEAD_DIM), lambda b, h: (b, k_blk0 + h)),
                  pl.BlockSpec((seq, HEAD_DIM), lambda b, h: (b, v_blk0 + h))],
        out_specs=pl.BlockSpec((seq, qw), lambda b, h: (b, h)),
        out_shape=jax.ShapeDtypeStruct((m, ATTN_WIDTH), BF16),
        scratch_shapes=[pltpu.VMEM((2, 2 * BLOCK, BLOCK), F32)],
        compiler_params=_params(2),
        name="swa_attention",
    )(sinks, qkv, qkv, qkv)


def _pool_kernel(p_ref, w_ref, sc_ref, o_ref, pad_s, y_s, *, seq, rows):
    grp = pl.program_id(1)
    gw = POOL_GROUP_WIDTH
    pad_s[0:POOL_HALO, :] = jnp.zeros((POOL_HALO, gw), F32)
    pad_s[POOL_HALO:, :] = p_ref[...]
    ext = rows + POOL_HALO
    lrow = lax.broadcasted_iota(jnp.int32, (rows, 128), 0)

    for gi, w in enumerate(POOL_WINDOWS):
        assert w & (w - 1) == 0 and w - 1 <= POOL_HALO

        @pl.when(grp == gi)
        def _(w=w):
            def body(r, carry):
                r0 = pl.multiple_of(r * rows, rows)
                cnt = jnp.minimum(lrow + (r0 + 1), w).astype(F32)
                for c in range(gw // 128):
                    x = pad_s[pl.ds(r0, ext), c * 128:(c + 1) * 128]
                    s = x
                    d = 1
                    while d < w:
                        s = s + pltpu.roll(s, d, 0)
                        d *= 2
                    y = s[POOL_HALO:] / cnt - x[POOL_HALO:]
                    y_s[pl.ds(r0, rows), c * 128:(c + 1) * 128] = y.astype(BF16)
                return carry

            lax.fori_loop(0, seq // rows, body, 0)

    out = jnp.dot(y_s[...], w_ref[0].astype(BF16), preferred_element_type=F32) * sc_ref[...]
    o_ref[...] = out.astype(o_ref.dtype)


def _pool(proj, pool_w, pool_scale, batch, seq, rows=128):
    m = proj.shape[0]
    gw = POOL_GROUP_WIDTH
    ng = len(POOL_WINDOWS)
    return pl.pallas_call(
        functools.partial(_pool_kernel, seq=seq, rows=rows),
        grid=(batch, ng),
        in_specs=[pl.BlockSpec((seq, gw), lambda b, g: (b, g)),
                  pl.BlockSpec((1, gw, gw), lambda b, g: (g, 0, 0)),
                  pl.BlockSpec((1, gw), lambda b, g: (0, g))],
        out_specs=pl.BlockSpec((seq, gw), lambda b, g: (b, g)),
        out_shape=jax.ShapeDtypeStruct((m, POOL_WIDTH), BF16),
        scratch_shapes=[pltpu.VMEM((seq + POOL_HALO, gw), F32), pltpu.VMEM((seq, gw), BF16)],
        compiler_params=_params(2),
        name="multiscale_pool",
    )(proj, pool_w, pool_scale)


def _rope_tables(seq):
    pos = jnp.arange(seq, dtype=F32)
    inv_freq = ROPE_THETA ** (-jnp.arange(0, HEAD_DIM, 2, dtype=F32) / HEAD_DIM)
    ang = pos[:, None] * inv_freq[None, :]
    cos, sin = jnp.cos(ang), jnp.sin(ang)
    cos = jnp.concatenate([cos, cos], axis=-1)
    sin = jnp.concatenate([-sin, sin], axis=-1)
    scale = HEAD_DIM ** -0.5
    return jnp.stack([cos * scale, cos]), jnp.stack([sin * scale, sin])


def kernel(x, ffn1_pre_g, ffn1_w_gate, ffn1_w_up, ffn1_w_down, ffn1_post_g, mix_pre_g, w_in, attn_sinks, pool_w, pool_scale, w_out, mix_post_g, ffn2_pre_g, ffn2_w_gate, ffn2_w_up, ffn2_w_down, ffn2_post_g):
    b, s, d = x.shape
    depth = ffn1_pre_g.shape[0]
    cos2, sin2 = _rope_tables(s)
    vec = lambda g: g.reshape(1, -1)
    xf = x.reshape(b * s, d)
    h = _prenorm(xf, vec(ffn1_pre_g[0]))
    for l in range(depth):
        y = _ffn(h, ffn1_w_gate[l], ffn1_w_up[l], ffn1_w_down[l])
        xf, h, qkv = _residual_qkv(xf, y, vec(ffn1_post_g[l]), vec(mix_pre_g[l]), w_in[l],
                                   cos2, sin2, s, FFN_RES_WEIGHT)
        a = _attention(qkv, attn_sinks[l], b, s)
        pm = _pool(_in_proj_pool(h, w_in[l]), pool_w[l], vec(pool_scale[l]), b, s)
        xf, h = _out_proj_residual(a, pm, xf, vec(mix_post_g[l]), vec(ffn2_pre_g[l]), w_out[l])

        y = _ffn(h, ffn2_w_gate[l], ffn2_w_up[l], ffn2_w_down[l])
        g_next = vec(ffn1_pre_g[l + 1]) if l + 1 < depth else None
        xf, h = _residual(xf, y, vec(ffn2_post_g[l]), g_next, FFN_RES_WEIGHT)
    return xf.reshape(b, s, d)
```

```python
import functools

import jax
import jax.numpy as jnp
from jax import lax
from jax.experimental import pallas as pl
from jax.experimental.pallas import tpu as pltpu

D_MODEL = 4096
HEAD_DIM = 128
N_HEADS = 16
N_KV_HEADS = 4
GQA_GROUP = N_HEADS // N_KV_HEADS
ATTN_WIDTH = N_HEADS * HEAD_DIM
KV_WIDTH = N_KV_HEADS * HEAD_DIM
BLOCK = 128
ROPE_THETA = 10000.0
POOL_WIDTH = D_MODEL - ATTN_WIDTH
POOL_WINDOWS = (2, 4, 8, 16)
POOL_GROUP_WIDTH = POOL_WIDTH // len(POOL_WINDOWS)
IN_PROJ_WIDTH = ATTN_WIDTH + 2 * KV_WIDTH + POOL_WIDTH
FFN_RES_WEIGHT = 0.5
RMS_EPS = 1e-6
POOL_HALO = 16

BF16 = jnp.bfloat16
F32 = jnp.float32
BF16_SUBLANES = 16

VMEM_LIMIT_BYTES = 56 * 1024 * 1024


def _params(n_axes):
    return pltpu.CompilerParams(
        dimension_semantics=("arbitrary",) * n_axes,
        vmem_limit_bytes=VMEM_LIMIT_BYTES,
    )


def _rms(x, g):
    ms = jnp.mean(x * x, axis=-1, keepdims=True)
    return x * lax.rsqrt(ms + RMS_EPS) * g


def _norm_kernel(x_ref, g_ref, h_ref):
    h_ref[...] = _rms(x_ref[...], g_ref[...]).astype(h_ref.dtype)


def _prenorm(x, g, tm=512):
    m, d = x.shape
    return pl.pallas_call(
        _norm_kernel,
        grid=(m // tm,),
        in_specs=[pl.BlockSpec((tm, d), lambda i: (i, 0)),
                  pl.BlockSpec((1, d), lambda i: (0, 0))],
        out_specs=pl.BlockSpec((tm, d), lambda i: (i, 0)),
        out_shape=jax.ShapeDtypeStruct((m, d), BF16),
        compiler_params=_params(1),
        name="prenorm",
    )(x, g)


def _residual_kernel(x_ref, y_ref, gpost_ref, *rest, weight, with_next):
    if with_next:
        gnext_ref, xo_ref, h_ref = rest
    else:
        (xo_ref,) = rest
    r = _rms(y_ref[...], gpost_ref[...])
    if weight != 1.0:
        r = weight * r
    xn = x_ref[...] + r
    xo_ref[...] = xn
    if with_next:
        h_ref[...] = _rms(xn, gnext_ref[...]).astype(h_ref.dtype)


def _residual(x, y, g_post, g_next, weight, tm=256):
    m, d = x.shape
    with_next = g_next is not None
    row = pl.BlockSpec((tm, d), lambda i: (i, 0))
    vec = pl.BlockSpec((1, d), lambda i: (0, 0))
    in_specs = [row, row, vec] + ([vec] if with_next else [])
    args = (x, y, g_post) + ((g_next,) if with_next else ())
    out_specs = [row] + ([row] if with_next else [])
    out_shape = [jax.ShapeDtypeStruct((m, d), F32)] + (
        [jax.ShapeDtypeStruct((m, d), BF16)] if with_next else [])
    outs = pl.pallas_call(
        functools.partial(_residual_kernel, weight=weight, with_next=with_next),
        grid=(m // tm,),
        in_specs=in_specs,
        out_specs=out_specs,
        out_shape=out_shape,
        compiler_params=_params(1),
        name="residual_norm",
    )(*args)
    return (outs[0], outs[1]) if with_next else (outs[0], None)


def _staged_kernel(*refs, n_x, n_w, n_extra, nj, slab, epilogue):
    x_refs = refs[:n_x]
    w_refs = refs[n_x:n_x + n_w]
    extra = refs[n_x + n_w:n_x + n_w + n_extra]
    o_ref = refs[n_x + n_w + n_extra]
    w_s = refs[n_x + n_w + n_extra + 1:]
    j = pl.program_id(0)
    i = pl.program_id(1)

    @pl.when(j < nj)
    def _():
        rows = pl.ds(pl.multiple_of(i * slab, slab), slab)
        for w_ref, s in zip(w_refs, w_s):
            s[j % 2, rows, :] = w_ref[...].astype(BF16)

    @pl.when(j > 0)
    def _():
        slot = (j + 1) % 2
        accs = []
        for s in w_s:
            acc, k0 = None, 0
            for x_ref in x_refs:
                kx = x_ref.shape[1]
                part = jnp.dot(x_ref[...], s[slot, k0:k0 + kx, :], preferred_element_type=F32)
                acc = part if acc is None else acc + part
                k0 += kx
            accs.append(acc)
        epilogue(accs, extra, o_ref)


def _staged_matmul(xs, ws, *, n, out_dtype, epilogue, name, tm, tn, col0=0,
                   extra=(), extra_specs=()):
    m = xs[0].shape[0]
    k = sum(x.shape[1] for x in xs)
    assert all(w.shape[0] == k for w in ws) and m % tm == 0 and col0 % tn == 0
    ni, nj = m // tm, pl.cdiv(n, tn)
    assert k % (ni * BF16_SUBLANES) == 0
    slab = k // ni
    cb = col0 // tn
    row_tile = lambda j, i: jnp.where(j == 0, 0, i)
    x_specs = [pl.BlockSpec((tm, x.shape[1]), lambda j, i: (row_tile(j, i), 0)) for x in xs]
    w_spec = pl.BlockSpec((slab, tn), lambda j, i: (jnp.where(j < nj, i, ni - 1),
                                                    cb + jnp.minimum(j, nj - 1)))
    o_spec = pl.BlockSpec((tm, tn), lambda j, i: (row_tile(j, i), jnp.maximum(j - 1, 0)))
    kern = functools.partial(_staged_kernel, n_x=len(xs), n_w=len(ws), n_extra=len(extra),
                             nj=nj, slab=slab, epilogue=epilogue)
    return pl.pallas_call(
        kern,
        grid=(nj + 1, ni),
        in_specs=x_specs + [w_spec] * len(ws) + list(extra_specs),
        out_specs=o_spec,
        out_shape=jax.ShapeDtypeStruct((m, n), out_dtype),
        scratch_shapes=[pltpu.VMEM((2, k, tn), BF16) for _ in ws],
        compiler_params=_params(2),
        name=name,
    )(*xs, *ws, *extra)


def _store_epilogue(accs, extra, o_ref):
    o_ref[...] = accs[0].astype(o_ref.dtype)


def _swiglu_epilogue(accs, extra, o_ref):
    g, u = accs
    o_ref[...] = (jax.nn.silu(g) * u).astype(o_ref.dtype)


def _rope(t, cos, sin_signed):
    return t * cos + pltpu.roll(t, HEAD_DIM // 2, 1) * sin_signed


def _ffn(h, w_gate, w_up, w_down):
    a = _staged_matmul([h], [w_gate, w_up], n=w_gate.shape[1], out_dtype=BF16,
                       epilogue=_swiglu_epilogue, name="ffn_gate_up", tm=1024, tn=512)
    return _staged_matmul([a], [w_down], n=w_down.shape[1], out_dtype=F32,
                          epilogue=_store_epilogue, name="ffn_down", tm=512, tn=512)


def _in_proj_pool(h, w, tm=1024, tn=512):
    return _staged_matmul([h], [w], n=POOL_WIDTH, out_dtype=F32, epilogue=_store_epilogue,
                          name="mix_in_proj_pool", tm=tm, tn=tn, col0=ATTN_WIDTH + 2 * KV_WIDTH)


def _stage_resident(t, ns, slab, w_ref, w_s):
    @pl.when(t < ns)
    def _():
        w_s[pl.ds(pl.multiple_of(t * slab, slab), slab), :] = w_ref[...].astype(BF16)


def _residual_qkv_kernel(x_ref, y_ref, gpost_ref, gnext_ref, w_ref, cos_ref, sin_ref,
                         xo_ref, h_ref, qkv_ref, w_s, *, ns, slab, weight):
    t = pl.program_id(0)
    _stage_resident(t, ns, slab, w_ref, w_s)

    @pl.when(t >= ns)
    def _():
        xn = x_ref[...] + weight * _rms(y_ref[...], gpost_ref[...])
        xo_ref[...] = xn
        h = _rms(xn, gnext_ref[...]).astype(BF16)
        h_ref[...] = h
        acc = jnp.dot(h, w_s[...], preferred_element_type=F32)
        for c in range(acc.shape[1] // HEAD_DIM):
            cols = slice(c * HEAD_DIM, (c + 1) * HEAD_DIM)
            if c * HEAD_DIM < ATTN_WIDTH + KV_WIDTH:
                kind = 0 if c * HEAD_DIM < ATTN_WIDTH else 1
                val = _rope(acc[:, cols], cos_ref[kind], sin_ref[kind])
            else:
                val = acc[:, cols]
            qkv_ref[:, cols] = val.astype(qkv_ref.dtype)


def _residual_qkv(x, y, g_post, g_next, w_in, cos2, sin2, seq, weight, tm=128, ns=32):
    m, d = x.shape
    n = ATTN_WIDTH + 2 * KV_WIDTH
    assert m % tm == 0 and seq % tm == 0 and d % (ns * BF16_SUBLANES) == 0
    slab = d // ns
    tile = lambda t: jnp.maximum(t - ns, 0)
    row = lambda width: pl.BlockSpec((tm, width), lambda t: (tile(t), 0))
    vec = pl.BlockSpec((1, d), lambda t: (0, 0))
    table = pl.BlockSpec((2, tm, HEAD_DIM), lambda t: (0, tile(t) % (seq // tm), 0))
    return pl.pallas_call(
        functools.partial(_residual_qkv_kernel, ns=ns, slab=slab, weight=weight),
        grid=(ns + m // tm,),
        in_specs=[row(d), row(d), vec, vec,
                  pl.BlockSpec((slab, n), lambda t: (jnp.minimum(t, ns - 1), 0)),
                  table, table],
        out_specs=[row(d), row(d), row(n)],
        out_shape=[jax.ShapeDtypeStruct((m, d), F32), jax.ShapeDtypeStruct((m, d), BF16),
                   jax.ShapeDtypeStruct((m, n), BF16)],
        scratch_shapes=[pltpu.VMEM((d, n), BF16)],
        compiler_params=_params(1),
        name="residual_qkv",
    )(x, y, g_post, g_next, w_in, cos2, sin2)


def _out_proj_residual_kernel(a_ref, p_ref, x_ref, gpost_ref, gnext_ref, w_ref,
                              xo_ref, h_ref, w_s, *, ns, slab):
    t = pl.program_id(0)
    _stage_resident(t, ns, slab, w_ref, w_s)

    @pl.when(t >= ns)
    def _():
        ka = a_ref.shape[1]
        y = (jnp.dot(a_ref[...], w_s[0:ka, :], preferred_element_type=F32)
             + jnp.dot(p_ref[...], w_s[ka:, :], preferred_element_type=F32))
        xn = x_ref[...] + _rms(y, gpost_ref[...])
        xo_ref[...] = xn
        h_ref[...] = _rms(xn, gnext_ref[...]).astype(h_ref.dtype)


def _out_proj_residual(a, pm, x, g_post, g_next, w, tm=128, ns=32):
    m, d = x.shape
    k = a.shape[1] + pm.shape[1]
    assert w.shape == (k, d) and m % tm == 0 and k % (ns * BF16_SUBLANES) == 0
    slab = k // ns
    row = lambda width: pl.BlockSpec((tm, width), lambda t: (jnp.maximum(t - ns, 0), 0))
    vec = pl.BlockSpec((1, d), lambda t: (0, 0))
    return pl.pallas_call(
        functools.partial(_out_proj_residual_kernel, ns=ns, slab=slab),
        grid=(ns + m // tm,),
        in_specs=[row(a.shape[1]), row(pm.shape[1]), row(d), vec, vec,
                  pl.BlockSpec((slab, d), lambda t: (jnp.minimum(t, ns - 1), 0))],
        out_specs=[row(d), row(d)],
        out_shape=[jax.ShapeDtypeStruct((m, d), F32), jax.ShapeDtypeStruct((m, d), BF16)],
        scratch_shapes=[pltpu.VMEM((k, d), BF16)],
        compiler_params=_params(1),
        name="out_proj_residual",
    )(a, pm, x, g_post, g_next, w)


def _attn_kernel(sinks_ref, q_ref, k_ref, v_ref, o_ref, bias_s, *, seq):
    kvh = pl.program_id(1)
    g, blk = GQA_GROUP, BLOCK
    key = lax.broadcasted_iota(jnp.int32, (2 * blk, blk), 0)
    qry = lax.broadcasted_iota(jnp.int32, (2 * blk, blk), 1)
    cur_ok = (key >= blk) & (key - blk <= qry)
    prev_ok = (key < blk) & (key > qry)
    bias_s[0] = jnp.where(cur_ok, 0.0, -jnp.inf)
    bias_s[1] = jnp.where(cur_ok | prev_ok, 0.0, -jnp.inf)
    sinks = [jnp.full((1, blk), sinks_ref[kvh * g + c], F32) for c in range(g)]
    contract_last = (((1,), (1,)), ((), ()))
    contract_first = (((0,), (0,)), ((), ()))

    def body(n, carry):
        r0 = pl.multiple_of(n * blk, blk)
        rp = pl.multiple_of(jnp.maximum(n - 1, 0) * blk, blk)
        kw = jnp.concatenate([k_ref[pl.ds(rp, blk), :], k_ref[pl.ds(r0, blk), :]], axis=0)
        vw = jnp.concatenate([v_ref[pl.ds(rp, blk), :], v_ref[pl.ds(r0, blk), :]], axis=0)
        bias = bias_s[jnp.minimum(n, 1)]
        for c in range(g):
            qc = q_ref[pl.ds(r0, blk), c * HEAD_DIM:(c + 1) * HEAD_DIM]
            s = lax.dot_general(kw, qc, contract_last, preferred_element_type=F32) + bias
            m = jnp.maximum(jnp.max(s, axis=0, keepdims=True), sinks[c])
            e = jnp.exp(s - m)
            denom = jnp.sum(e, axis=0, keepdims=True) + jnp.exp(sinks[c] - m)
            p = (e * (1.0 / denom)).astype(BF16)
            o = lax.dot_general(p, vw, contract_first, preferred_element_type=F32)
            o_ref[pl.ds(r0, blk), c * HEAD_DIM:(c + 1) * HEAD_DIM] = o.astype(o_ref.dtype)
        return carry

    lax.fori_loop(0, seq // blk, body, 0, unroll=8)


def _attention(qkv, sinks, batch, seq):
    m = qkv.shape[0]
    qw = GQA_GROUP * HEAD_DIM
    k_blk0 = ATTN_WIDTH // HEAD_DIM
    v_blk0 = (ATTN_WIDTH + KV_WIDTH) // HEAD_DIM
    return pl.pallas_call(
        functools.partial(_attn_kernel, seq=seq),
        grid=(batch, N_KV_HEADS),
        in_specs=[pl.BlockSpec(memory_space=pltpu.SMEM),
                  pl.BlockSpec((seq, qw), lambda b, h: (b, h)),
                  pl.BlockSpec((seq, HEAD_DIM), lambda b, h: (b, k_blk0 + h)),
                  pl.BlockSpec((seq, HEAD_DIM), lambda b, h: (b, v_blk0 + h))],
        out_specs=pl.BlockSpec((seq, qw), lambda b, h: (b, h)),
        out_shape=jax.ShapeDtypeStruct((m, ATTN_WIDTH), BF16),
        scratch_shapes=[pltpu.VMEM((2, 2 * BLOCK, BLOCK), F32)],
        compiler_params=_params(2),
        name="swa_attention",
    )(sinks, qkv, qkv, qkv)


def _pool_kernel(p_ref, w_ref, sc_ref, o_ref, pad_s, y_s, *, seq, rows):
    grp = pl.program_id(1)
    gw = POOL_GROUP_WIDTH
    pad_s[0:POOL_HALO, :] = jnp.zeros((POOL_HALO, gw), F32)
    pad_s[POOL_HALO:, :] = p_ref[...]
    ext = rows + POOL_HALO
    lrow = lax.broadcasted_iota(jnp.int32, (rows, 128), 0)

    for gi, w in enumerate(POOL_WINDOWS):
        assert w & (w - 1) == 0 and w - 1 <= POOL_HALO

        @pl.when(grp == gi)
        def _(w=w):
            def body(r, carry):
                r0 = pl.multiple_of(r * rows, rows)
                cnt = jnp.minimum(lrow + (r0 + 1), w).astype(F32)
                for c in range(gw // 128):
                    x = pad_s[pl.ds(r0, ext), c * 128:(c + 1) * 128]
                    s = x
                    d = 1
                    while d < w:
                        s = s + pltpu.roll(s, d, 0)
                        d *= 2
                    y = s[POOL_HALO:] / cnt - x[POOL_HALO:]
                    y_s[pl.ds(r0, rows), c * 128:(c + 1) * 128] = y.astype(BF16)
                return carry

            lax.fori_loop(0, seq // rows, body, 0)

    out = jnp.dot(y_s[...], w_ref[0].astype(BF16), preferred_element_type=F32) * sc_ref[...]
    o_ref[...] = out.astype(o_ref.dtype)


def _pool(proj, pool_w, pool_scale, batch, seq, rows=128):
    m = proj.shape[0]
    gw = POOL_GROUP_WIDTH
    ng = len(POOL_WINDOWS)
    return pl.pallas_call(
        functools.partial(_pool_kernel, seq=seq, rows=rows),
        grid=(batch, ng),
        in_specs=[pl.BlockSpec((seq, gw), lambda b, g: (b, g)),
                  pl.BlockSpec((1, gw, gw), lambda b, g: (g, 0, 0)),
                  pl.BlockSpec((1, gw), lambda b, g: (0, g))],
        out_specs=pl.BlockSpec((seq, gw), lambda b, g: (b, g)),
        out_shape=jax.ShapeDtypeStruct((m, POOL_WIDTH), BF16),
        scratch_shapes=[pltpu.VMEM((seq + POOL_HALO, gw), F32), pltpu.VMEM((seq, gw), BF16)],
        compiler_params=_params(2),
        name="multiscale_pool",
    )(proj, pool_w, pool_scale)


def _rope_tables(seq):
    pos = jnp.arange(seq, dtype=F32)
    inv_freq = ROPE_THETA ** (-jnp.arange(0, HEAD_DIM, 2, dtype=F32) / HEAD_DIM)
    ang = pos[:, None] * inv_freq[None, :]
    cos, sin = jnp.cos(ang), jnp.sin(ang)
    cos = jnp.concatenate([cos, cos], axis=-1)
    sin = jnp.concatenate([-sin, sin], axis=-1)
    scale = HEAD_DIM ** -0.5
    return jnp.stack([cos * scale, cos]), jnp.stack([sin * scale, sin])


def kernel(x, ffn1_pre_g, ffn1_w_gate, ffn1_w_up, ffn1_w_down, ffn1_post_g, mix_pre_g, w_in, attn_sinks, pool_w, pool_scale, w_out, mix_post_g, ffn2_pre_g, ffn2_w_gate, ffn2_w_up, ffn2_w_down, ffn2_post_g):
    b, s, d = x.shape
    depth = ffn1_pre_g.shape[0]
    cos2, sin2 = _rope_tables(s)
    vec = lambda g: g.reshape(1, -1)
    xf = x.reshape(b * s, d)
    h = _prenorm(xf, vec(ffn1_pre_g[0]))
    for l in range(depth):
        y = _ffn(h, ffn1_w_gate[l], ffn1_w_up[l], ffn1_w_down[l])
        xf, h, qkv = _residual_qkv(xf, y, vec(ffn1_post_g[l]), vec(mix_pre_g[l]), w_in[l],
                                   cos2, sin2, s, FFN_RES_WEIGHT)
        a = _attention(qkv, attn_sinks[l], b, s)
        pm = _pool(_in_proj_pool(h, w_in[l]), pool_w[l], vec(pool_scale[l]), b, s)
        xf, h = _out_proj_residual(a, pm, xf, vec(mix_post_g[l]), vec(ffn2_pre_g[l]), w_out[l])

        y = _ffn(h, ffn2_w_gate[l], ffn2_w_up[l], ffn2_w_down[l])
        g_next = vec(ffn1_pre_g[l + 1]) if l + 1 < depth else None
        xf, h = _residual(xf, y, vec(ffn2_post_g[l]), g_next, FFN_RES_WEIGHT)
    return xf.reshape(b, s, d)
```

```python
import functools

import jax
import jax.numpy as jnp
from jax import lax
from jax.experimental import pallas as pl
from jax.experimental.pallas import tpu as pltpu

D_MODEL = 4096
HEAD_DIM = 128
N_HEADS = 16
N_KV_HEADS = 4
GQA_GROUP = N_HEADS // N_KV_HEADS
ATTN_WIDTH = N_HEADS * HEAD_DIM
KV_WIDTH = N_KV_HEADS * HEAD_DIM
BLOCK = 128
ROPE_THETA = 10000.0
POOL_WIDTH = D_MODEL - ATTN_WIDTH
POOL_WINDOWS = (2, 4, 8, 16)
POOL_GROUP_WIDTH = POOL_WIDTH // len(POOL_WINDOWS)
IN_PROJ_WIDTH = ATTN_WIDTH + 2 * KV_WIDTH + POOL_WIDTH
FFN_RES_WEIGHT = 0.5
RMS_EPS = 1e-6
POOL_HALO = 16

BF16 = jnp.bfloat16
F32 = jnp.float32
BF16_SUBLANES = 16

VMEM_LIMIT_BYTES = 56 * 1024 * 1024


def _params(n_axes):
    return pltpu.CompilerParams(
        dimension_semantics=("arbitrary",) * n_axes,
        vmem_limit_bytes=VMEM_LIMIT_BYTES,
    )


def _rms(x, g):
    ms = jnp.mean(x * x, axis=-1, keepdims=True)
    return x * lax.rsqrt(ms + RMS_EPS) * g


def _norm_kernel(x_ref, g_ref, h_ref):
    h_ref[...] = _rms(x_ref[...], g_ref[...]).astype(h_ref.dtype)


def _prenorm(x, g, tm=512):
    m, d = x.shape
    return pl.pallas_call(
        _norm_kernel,
        grid=(m // tm,),
        in_specs=[pl.BlockSpec((tm, d), lambda i: (i, 0)),
                  pl.BlockSpec((1, d), lambda i: (0, 0))],
        out_specs=pl.BlockSpec((tm, d), lambda i: (i, 0)),
        out_shape=jax.ShapeDtypeStruct((m, d), BF16),
        compiler_params=_params(1),
        name="prenorm",
    )(x, g)


def _residual_kernel(x_ref, y_ref, gpost_ref, *rest, weight, with_next):
    if with_next:
        gnext_ref, xo_ref, h_ref = rest
    else:
        (xo_ref,) = rest
    r = _rms(y_ref[...], gpost_ref[...])
    if weight != 1.0:
        r = weight * r
    xn = x_ref[...] + r
    xo_ref[...] = xn
    if with_next:
        h_ref[...] = _rms(xn, gnext_ref[...]).astype(h_ref.dtype)


def _residual(x, y, g_post, g_next, weight, tm=256):
    m, d = x.shape
    with_next = g_next is not None
    row = pl.BlockSpec((tm, d), lambda i: (i, 0))
    vec = pl.BlockSpec((1, d), lambda i: (0, 0))
    in_specs = [row, row, vec] + ([vec] if with_next else [])
    args = (x, y, g_post) + ((g_next,) if with_next else ())
    out_specs = [row] + ([row] if with_next else [])
    out_shape = [jax.ShapeDtypeStruct((m, d), F32)] + (
        [jax.ShapeDtypeStruct((m, d), BF16)] if with_next else [])
    outs = pl.pallas_call(
        functools.partial(_residual_kernel, weight=weight, with_next=with_next),
        grid=(m // tm,),
        in_specs=in_specs,
        out_specs=out_specs,
        out_shape=out_shape,
        compiler_params=_params(1),
        name="residual_norm",
    )(*args)
    return (outs[0], outs[1]) if with_next else (outs[0], None)


def _staged_kernel(*refs, n_x, n_w, n_extra, nj, slab, epilogue):
    x_refs = refs[:n_x]
    w_refs = refs[n_x:n_x + n_w]
    extra = refs[n_x + n_w:n_x + n_w + n_extra]
    o_ref = refs[n_x + n_w + n_extra]
    w_s = refs[n_x + n_w + n_extra + 1:]
    j = pl.program_id(0)
    i = pl.program_id(1)

    @pl.when(j < nj)
    def _():
        rows = pl.ds(pl.multiple_of(i * slab, slab), slab)
        for w_ref, s in zip(w_refs, w_s):
            s[j % 2, rows, :] = w_ref[...].astype(BF16)

    @pl.when(j > 0)
    def _():
        slot = (j + 1) % 2
        accs = []
        for s in w_s:
            acc, k0 = None, 0
            for x_ref in x_refs:
                kx = x_ref.shape[1]
                part = jnp.dot(x_ref[...], s[slot, k0:k0 + kx, :], preferred_element_type=F32)
                acc = part if acc is None else acc + part
                k0 += kx
            accs.append(acc)
        epilogue(accs, extra, o_ref)


def _staged_matmul(xs, ws, *, n, out_dtype, epilogue, name, tm, tn, col0=0,
                   extra=(), extra_specs=()):
    m = xs[0].shape[0]
    k = sum(x.shape[1] for x in xs)
    assert all(w.shape[0] == k for w in ws) and m % tm == 0 and col0 % tn == 0
    ni, nj = m // tm, pl.cdiv(n, tn)
    assert k % (ni * BF16_SUBLANES) == 0
    slab = k // ni
    cb = col0 // tn
    row_tile = lambda j, i: jnp.where(j == 0, 0, i)
    x_specs = [pl.BlockSpec((tm, x.shape[1]), lambda j, i: (row_tile(j, i), 0)) for x in xs]
    w_spec = pl.BlockSpec((slab, tn), lambda j, i: (jnp.where(j < nj, i, ni - 1),
                                                    cb + jnp.minimum(j, nj - 1)))
    o_spec = pl.BlockSpec((tm, tn), lambda j, i: (row_tile(j, i), jnp.maximum(j - 1, 0)))
    kern = functools.partial(_staged_kernel, n_x=len(xs), n_w=len(ws), n_extra=len(extra),
                             nj=nj, slab=slab, epilogue=epilogue)
    return pl.pallas_call(
        kern,
        grid=(nj + 1, ni),
        in_specs=x_specs + [w_spec] * len(ws) + list(extra_specs),
        out_specs=o_spec,
        out_shape=jax.ShapeDtypeStruct((m, n), out_dtype),
        scratch_shapes=[pltpu.VMEM((2, k, tn), BF16) for _ in ws],
        compiler_params=_params(2),
        name=name,
    )(*xs, *ws, *extra)


def _store_epilogue(accs, extra, o_ref):
    o_ref[...] = accs[0].astype(o_ref.dtype)


def _swiglu_epilogue(accs, extra, o_ref):
    g, u = accs
    o_ref[...] = (jax.nn.silu(g) * u).astype(o_ref.dtype)


def _rope(t, cos, sin_signed):
    return t * cos + pltpu.roll(t, HEAD_DIM // 2, 1) * sin_signed


def _ffn(h, w_gate, w_up, w_down):
    a = _staged_matmul([h], [w_gate, w_up], n=w_gate.shape[1], out_dtype=BF16,
                       epilogue=_swiglu_epilogue, name="ffn_gate_up", tm=1024, tn=512)
    return _staged_matmul([a], [w_down], n=w_down.shape[1], out_dtype=F32,
                          epilogue=_store_epilogue, name="ffn_down", tm=512, tn=512)


def _in_proj_pool(h, w, tm=1024, tn=512):
    return _staged_matmul([h], [w], n=POOL_WIDTH, out_dtype=F32, epilogue=_store_epilogue,
                          name="mix_in_proj_pool", tm=tm, tn=tn, col0=ATTN_WIDTH + 2 * KV_WIDTH)


def _stage_resident(t, ns, slab, w_ref, w_s, handover_s):
    @pl.when(t == 0)
    def _():
        handover_s[...] = jnp.zeros(handover_s.shape, handover_s.dtype)

    @pl.when(t < ns)
    def _():
        w_s[pl.ds(pl.multiple_of(t * slab, slab), slab), :] = w_ref[...].astype(BF16)


def _residual_qkv_kernel(x_ref, y_ref, gpost_ref, gnext_ref, w_ref, cos_ref, sin_ref,
                         xo_ref, h_ref, qkv_ref, w_s, h_s, *, ns, slab, weight):
    t = pl.program_id(0)
    _stage_resident(t, ns, slab, w_ref, w_s, h_s)

    @pl.when(t >= ns)
    def _():
        k = t - ns
        acc = jnp.dot(h_s[(k + 1) % 2], w_s[...], preferred_element_type=F32)
        for c in range(acc.shape[1] // HEAD_DIM):
            cols = slice(c * HEAD_DIM, (c + 1) * HEAD_DIM)
            if c * HEAD_DIM < ATTN_WIDTH + KV_WIDTH:
                kind = 0 if c * HEAD_DIM < ATTN_WIDTH else 1
                val = _rope(acc[:, cols], cos_ref[kind], sin_ref[kind])
            else:
                val = acc[:, cols]
            qkv_ref[:, cols] = val.astype(qkv_ref.dtype)

        xn = x_ref[...] + weight * _rms(y_ref[...], gpost_ref[...])
        xo_ref[...] = xn
        h = _rms(xn, gnext_ref[...]).astype(BF16)
        h_ref[...] = h
        h_s[k % 2] = h


def _residual_qkv(x, y, g_post, g_next, w_in, cos2, sin2, seq, weight, tm=128, ns=32):
    m, d = x.shape
    n = ATTN_WIDTH + 2 * KV_WIDTH
    assert m % tm == 0 and seq % tm == 0 and d % (ns * BF16_SUBLANES) == 0
    slab = d // ns
    nt = m // tm
    tile = lambda t, lag: jnp.clip(t - ns - lag, 0, nt - 1)
    row = lambda width, lag: pl.BlockSpec((tm, width), lambda t: (tile(t, lag), 0))
    vec = pl.BlockSpec((1, d), lambda t: (0, 0))
    table = pl.BlockSpec((2, tm, HEAD_DIM), lambda t: (0, tile(t, 1) % (seq // tm), 0))
    return pl.pallas_call(
        functools.partial(_residual_qkv_kernel, ns=ns, slab=slab, weight=weight),
        grid=(ns + nt + 1,),
        in_specs=[row(d, 0), row(d, 0), vec, vec,
                  pl.BlockSpec((slab, n), lambda t: (jnp.minimum(t, ns - 1), 0)),
                  table, table],
        out_specs=[row(d, 0), row(d, 0), row(n, 1)],
        out_shape=[jax.ShapeDtypeStruct((m, d), F32), jax.ShapeDtypeStruct((m, d), BF16),
                   jax.ShapeDtypeStruct((m, n), BF16)],
        scratch_shapes=[pltpu.VMEM((d, n), BF16), pltpu.VMEM((2, tm, d), BF16)],
        compiler_params=_params(1),
        name="residual_qkv",
    )(x, y, g_post, g_next, w_in, cos2, sin2)


def _out_proj_residual_kernel(a_ref, p_ref, x_ref, gpost_ref, gnext_ref, w_ref,
                              xo_ref, h_ref, w_s, y_s, *, ns, slab):
    t = pl.program_id(0)
    _stage_resident(t, ns, slab, w_ref, w_s, y_s)

    @pl.when(t >= ns)
    def _():
        k = t - ns
        ka = a_ref.shape[1]
        y = (jnp.dot(a_ref[...], w_s[0:ka, :], preferred_element_type=F32)
             + jnp.dot(p_ref[...], w_s[ka:, :], preferred_element_type=F32))
        xn = x_ref[...] + _rms(y_s[(k + 1) % 2], gpost_ref[...])
        xo_ref[...] = xn
        h_ref[...] = _rms(xn, gnext_ref[...]).astype(h_ref.dtype)
        y_s[k % 2] = y


def _out_proj_residual(a, pm, x, g_post, g_next, w, tm=128, ns=32):
    m, d = x.shape
    k = a.shape[1] + pm.shape[1]
    assert w.shape == (k, d) and m % tm == 0 and k % (ns * BF16_SUBLANES) == 0
    slab = k // ns
    nt = m // tm
    row = lambda width, lag: pl.BlockSpec(
        (tm, width), lambda t: (jnp.clip(t - ns - lag, 0, nt - 1), 0))
    vec = pl.BlockSpec((1, d), lambda t: (0, 0))
    return pl.pallas_call(
        functools.partial(_out_proj_residual_kernel, ns=ns, slab=slab),
        grid=(ns + nt + 1,),
        in_specs=[row(a.shape[1], 0), row(pm.shape[1], 0), row(d, 1), vec, vec,
                  pl.BlockSpec((slab, d), lambda t: (jnp.minimum(t, ns - 1), 0))],
        out_specs=[row(d, 1), row(d, 1)],
        out_shape=[jax.ShapeDtypeStruct((m, d), F32), jax.ShapeDtypeStruct((m, d), BF16)],
        scratch_shapes=[pltpu.VMEM((k, d), BF16), pltpu.VMEM((2, tm, d), F32)],
        compiler_params=_params(1),
        name="out_proj_residual",
    )(a, pm, x, g_post, g_next, w)


def _attn_kernel(sinks_ref, q_ref, k_ref, v_ref, o_ref, bias_s, *, seq):
    kvh = pl.program_id(1)
    g, blk = GQA_GROUP, BLOCK
    key = lax.broadcasted_iota(jnp.int32, (2 * blk, blk), 0)
    qry = lax.broadcasted_iota(jnp.int32, (2 * blk, blk), 1)
    cur_ok = (key >= blk) & (key - blk <= qry)
    prev_ok = (key < blk) & (key > qry)
    bias_s[0] = jnp.where(cur_ok, 0.0, -jnp.inf)
    bias_s[1] = jnp.where(cur_ok | prev_ok, 0.0, -jnp.inf)
    sinks = [jnp.full((1, blk), sinks_ref[kvh * g + c], F32) for c in range(g)]
    contract_last = (((1,), (1,)), ((), ()))
    contract_first = (((0,), (0,)), ((), ()))

    def body(n, carry):
        r0 = pl.multiple_of(n * blk, blk)
        rp = pl.multiple_of(jnp.maximum(n - 1, 0) * blk, blk)
        kw = jnp.concatenate([k_ref[pl.ds(rp, blk), :], k_ref[pl.ds(r0, blk), :]], axis=0)
        vw = jnp.concatenate([v_ref[pl.ds(rp, blk), :], v_ref[pl.ds(r0, blk), :]], axis=0)
        bias = bias_s[jnp.minimum(n, 1)]
        for c in range(g):
            qc = q_ref[pl.ds(r0, blk), c * HEAD_DIM:(c + 1) * HEAD_DIM]
            s = lax.dot_general(kw, qc, contract_last, preferred_element_type=F32) + bias
            m = jnp.maximum(jnp.max(s, axis=0, keepdims=True), sinks[c])
            e = jnp.exp(s - m)
            denom = jnp.sum(e, axis=0, keepdims=True) + jnp.exp(sinks[c] - m)
            p = (e * (1.0 / denom)).astype(BF16)
            o = lax.dot_general(p, vw, contract_first, preferred_element_type=F32)
            o_ref[pl.ds(r0, blk), c * HEAD_DIM:(c + 1) * HEAD_DIM] = o.astype(o_ref.dtype)
        return carry

    lax.fori_loop(0, seq // blk, body, 0, unroll=16)


def _attention(qkv, sinks, batch, seq):
    m = qkv.shape[0]
    qw = GQA_GROUP * HEAD_DIM
    k_blk0 = ATTN_WIDTH // HEAD_DIM
    v_blk0 = (ATTN_WIDTH + KV_WIDTH) // HEAD_DIM
    return pl.pallas_call(
        functools.partial(_attn_kernel, seq=seq),
        grid=(batch, N_KV_HEADS),
        in_specs=[pl.BlockSpec(memory_space=pltpu.SMEM),
                  pl.BlockSpec((seq, qw), lambda b, h: (b, h)),
                  pl.BlockSpec((seq, HEAD_DIM), lambda b, h: (b, k_blk0 + h)),
                  pl.BlockSpec((seq, HEAD_DIM), lambda b, h: (b, v_blk0 + h))],
        out_specs=pl.BlockSpec((seq, qw), lambda b, h: (b, h)),
        out_shape=jax.ShapeDtypeStruct((m, ATTN_WIDTH), BF16),
        scratch_shapes=[pltpu.VMEM((2, 2 * BLOCK, BLOCK), F32)],
        compiler_params=_params(2),
        name="swa_attention",
    )(sinks, qkv, qkv, qkv)


def _pool_kernel(p_ref, w_ref, sc_ref, o_ref, pad_s, y_s, *, seq, rows):
    grp = pl.program_id(1)
    gw = POOL_GROUP_WIDTH
    pad_s[0:POOL_HALO, :] = jnp.zeros((POOL_HALO, gw), F32)
    pad_s[POOL_HALO:, :] = p_ref[...]
    ext = rows + POOL_HALO
    lrow = lax.broadcasted_iota(jnp.int32, (rows, 128), 0)

    for gi, w in enumerate(POOL_WINDOWS):
        assert w & (w - 1) == 0 and w - 1 <= POOL_HALO

        @pl.when(grp == gi)
        def _(w=w):
            def body(r, carry):
                r0 = pl.multiple_of(r * rows, rows)
                cnt = jnp.minimum(lrow + (r0 + 1), w).astype(F32)
                for c in range(gw // 128):
                    x = pad_s[pl.ds(r0, ext), c * 128:(c + 1) * 128]
                    s = x
                    d = 1
                    while d < w:
                        s = s + pltpu.roll(s, d, 0)
                        d *= 2
                    y = s[POOL_HALO:] / cnt - x[POOL_HALO:]
                    y_s[pl.ds(r0, rows), c * 128:(c + 1) * 128] = y.astype(BF16)
                return carry

            lax.fori_loop(0, seq // rows, body, 0)

    out = jnp.dot(y_s[...], w_ref[0].astype(BF16), preferred_element_type=F32) * sc_ref[...]
    o_ref[...] = out.astype(o_ref.dtype)


def _pool(proj, pool_w, pool_scale, batch, seq, rows=128):
    m = proj.shape[0]
    gw = POOL_GROUP_WIDTH
    ng = len(POOL_WINDOWS)
    return pl.pallas_call(
        functools.partial(_pool_kernel, seq=seq, rows=rows),
        grid=(batch, ng),
        in_specs=[pl.BlockSpec((seq, gw), lambda b, g: (b, g)),
                  pl.BlockSpec((1, gw, gw), lambda b, g: (g, 0, 0)),
                  pl.BlockSpec((1, gw), lambda b, g: (0, g))],
        out_specs=pl.BlockSpec((seq, gw), lambda b, g: (b, g)),
        out_shape=jax.ShapeDtypeStruct((m, POOL_WIDTH), BF16),
        scratch_shapes=[pltpu.VMEM((seq + POOL_HALO, gw), F32), pltpu.VMEM((seq, gw), BF16)],
        compiler_params=_params(2),
        name="multiscale_pool",
    )(proj, pool_w, pool_scale)


def _rope_tables(seq):
    pos = jnp.arange(seq, dtype=F32)
    inv_freq = ROPE_THETA ** (-jnp.arange(0, HEAD_DIM, 2, dtype=F32) / HEAD_DIM)
    ang = pos[:, None] * inv_freq[None, :]
    cos, sin = jnp.cos(ang), jnp.sin(ang)
    cos = jnp.concatenate([cos, cos], axis=-1)
    sin = jnp.concatenate([-sin, sin], axis=-1)
    scale = HEAD_DIM ** -0.5
    return jnp.stack([cos * scale, cos]), jnp.stack([sin * scale, sin])


def kernel(x, ffn1_pre_g, ffn1_w_gate, ffn1_w_up, ffn1_w_down, ffn1_post_g, mix_pre_g, w_in, attn_sinks, pool_w, pool_scale, w_out, mix_post_g, ffn2_pre_g, ffn2_w_gate, ffn2_w_up, ffn2_w_down, ffn2_post_g):
    b, s, d = x.shape
    depth = ffn1_pre_g.shape[0]
    cos2, sin2 = _rope_tables(s)
    vec = lambda g: g.reshape(1, -1)
    xf = x.reshape(b * s, d)
    h = _prenorm(xf, vec(ffn1_pre_g[0]))
    for l in range(depth):
        y = _ffn(h, ffn1_w_gate[l], ffn1_w_up[l], ffn1_w_down[l])
        xf, h, qkv = _residual_qkv(xf, y, vec(ffn1_post_g[l]), vec(mix_pre_g[l]), w_in[l],
                                   cos2, sin2, s, FFN_RES_WEIGHT)
        a = _attention(qkv, attn_sinks[l], b, s)
        pm = _pool(_in_proj_pool(h, w_in[l]), pool_w[l], vec(pool_scale[l]), b, s)
        xf, h = _out_proj_residual(a, pm, xf, vec(mix_post_g[l]), vec(ffn2_pre_g[l]), w_out[l])

        y = _ffn(h, ffn2_w_gate[l], ffn2_w_up[l], ffn2_w_down[l])
        g_next = vec(ffn1_pre_g[l + 1]) if l + 1 < depth else None
        xf, h = _residual(xf, y, vec(ffn2_post_g[l]), g_next, FFN_RES_WEIGHT)
    return xf.reshape(b, s, d)
```

```python
import functools

import jax
import jax.numpy as jnp
from jax import lax
from jax.experimental import pallas as pl
from jax.experimental.pallas import tpu as pltpu

D_MODEL = 4096
HEAD_DIM = 128
N_HEADS = 16
N_KV_HEADS = 4
GQA_GROUP = N_HEADS // N_KV_HEADS
ATTN_WIDTH = N_HEADS * HEAD_DIM
KV_WIDTH = N_KV_HEADS * HEAD_DIM
BLOCK = 128
ROPE_THETA = 10000.0
POOL_WIDTH = D_MODEL - ATTN_WIDTH
POOL_WINDOWS = (2, 4, 8, 16)
POOL_GROUP_WIDTH = POOL_WIDTH // len(POOL_WINDOWS)
IN_PROJ_WIDTH = ATTN_WIDTH + 2 * KV_WIDTH + POOL_WIDTH
FFN_RES_WEIGHT = 0.5
RMS_EPS = 1e-6
POOL_HALO = 16

BF16 = jnp.bfloat16
F32 = jnp.float32
BF16_SUBLANES = 16
SIDE_CAST_ROWS = 32

VMEM_LIMIT_BYTES = 56 * 1024 * 1024


def _params(n_axes):
    return pltpu.CompilerParams(
        dimension_semantics=("arbitrary",) * n_axes,
        vmem_limit_bytes=VMEM_LIMIT_BYTES,
    )


def _rms(x, g):
    ms = jnp.mean(x * x, axis=-1, keepdims=True)
    return x * lax.rsqrt(ms + RMS_EPS) * g


def _norm_kernel(x_ref, g_ref, h_ref):
    h_ref[...] = _rms(x_ref[...], g_ref[...]).astype(h_ref.dtype)


def _prenorm(x, g, tm=512):
    m, d = x.shape
    return pl.pallas_call(
        _norm_kernel,
        grid=(m // tm,),
        in_specs=[pl.BlockSpec((tm, d), lambda i: (i, 0)),
                  pl.BlockSpec((1, d), lambda i: (0, 0))],
        out_specs=pl.BlockSpec((tm, d), lambda i: (i, 0)),
        out_shape=jax.ShapeDtypeStruct((m, d), BF16),
        compiler_params=_params(1),
        name="prenorm",
    )(x, g)


def _residual_kernel(x_ref, y_ref, gpost_ref, *rest, weight, with_next):
    if with_next:
        gnext_ref, xo_ref, h_ref = rest
    else:
        (xo_ref,) = rest
    r = _rms(y_ref[...], gpost_ref[...])
    if weight != 1.0:
        r = weight * r
    xn = x_ref[...] + r
    xo_ref[...] = xn
    if with_next:
        h_ref[...] = _rms(xn, gnext_ref[...]).astype(h_ref.dtype)


def _residual(x, y, g_post, g_next, weight, tm=256):
    m, d = x.shape
    with_next = g_next is not None
    row = pl.BlockSpec((tm, d), lambda i: (i, 0))
    vec = pl.BlockSpec((1, d), lambda i: (0, 0))
    in_specs = [row, row, vec] + ([vec] if with_next else [])
    args = (x, y, g_post) + ((g_next,) if with_next else ())
    out_specs = [row] + ([row] if with_next else [])
    out_shape = [jax.ShapeDtypeStruct((m, d), F32)] + (
        [jax.ShapeDtypeStruct((m, d), BF16)] if with_next else [])
    outs = pl.pallas_call(
        functools.partial(_residual_kernel, weight=weight, with_next=with_next),
        grid=(m // tm,),
        in_specs=in_specs,
        out_specs=out_specs,
        out_shape=out_shape,
        compiler_params=_params(1),
        name="residual_norm",
    )(*args)
    return (outs[0], outs[1]) if with_next else (outs[0], None)


def _staged_kernel(*refs, n_x, n_w, n_extra, side_steps, nj, slab, epilogue):
    n_side = len(side_steps)
    n_in = n_x + n_w + n_extra
    x_refs = refs[:n_x]
    w_refs = refs[n_x:n_x + n_w]
    extra = refs[n_x + n_w:n_in]
    side_in = refs[n_in:n_in + n_side]
    o_ref = refs[n_in + n_side]
    side_out = refs[n_in + n_side + 1:n_in + 2 * n_side + 1]
    w_s = refs[n_in + 2 * n_side + 1:]
    j = pl.program_id(0)
    i = pl.program_id(1)

    for src, dst, steps in zip(side_in, side_out, side_steps):
        @pl.when(j * pl.num_programs(1) + i < steps)
        def _(src=src, dst=dst):
            dst[...] = src[...].astype(dst.dtype)

    @pl.when(j < nj)
    def _():
        rows = pl.ds(pl.multiple_of(i * slab, slab), slab)
        for w_ref, s in zip(w_refs, w_s):
            s[j % 2, rows, :] = w_ref[...].astype(BF16)

    @pl.when(j > 0)
    def _():
        slot = (j + 1) % 2
        accs = []
        for s in w_s:
            acc, k0 = None, 0
            for x_ref in x_refs:
                kx = x_ref.shape[1]
                part = jnp.dot(x_ref[...], s[slot, k0:k0 + kx, :], preferred_element_type=F32)
                acc = part if acc is None else acc + part
                k0 += kx
            accs.append(acc)
        epilogue(accs, extra, o_ref)


def _staged_matmul(xs, ws, *, n, out_dtype, epilogue, name, tm, tn, col0=0,
                   extra=(), extra_specs=(), side_casts=()):
    m = xs[0].shape[0]
    k = sum(x.shape[1] for x in xs)
    assert all(w.shape[0] == k for w in ws) and m % tm == 0 and col0 % tn == 0
    ni, nj = m // tm, pl.cdiv(n, tn)
    assert k % (ni * BF16_SUBLANES) == 0
    slab = k // ni
    cb = col0 // tn
    side_in, side_specs, side_shapes, side_steps = [], [], [], []
    for arr, rows, cols in side_casts:
        steps = arr.shape[0] // rows
        assert arr.shape[0] % rows == 0 and rows % BF16_SUBLANES == 0 and steps <= (nj + 1) * ni
        side_in.append(arr)
        side_specs.append(pl.BlockSpec(
            (rows, cols), lambda j, i, steps=steps: (jnp.minimum(j * ni + i, steps - 1), 0)))
        side_shapes.append(jax.ShapeDtypeStruct((arr.shape[0], cols), BF16))
        side_steps.append(steps)
    row_tile = lambda j, i: jnp.where(j == 0, 0, i)
    x_specs = [pl.BlockSpec((tm, x.shape[1]), lambda j, i: (row_tile(j, i), 0)) for x in xs]
    w_spec = pl.BlockSpec((slab, tn), lambda j, i: (jnp.where(j < nj, i, ni - 1),
                                                    cb + jnp.minimum(j, nj - 1)))
    o_spec = pl.BlockSpec((tm, tn), lambda j, i: (row_tile(j, i), jnp.maximum(j - 1, 0)))
    kern = functools.partial(_staged_kernel, n_x=len(xs), n_w=len(ws), n_extra=len(extra),
                             side_steps=tuple(side_steps), nj=nj, slab=slab, epilogue=epilogue)
    outs = pl.pallas_call(
        kern,
        grid=(nj + 1, ni),
        in_specs=x_specs + [w_spec] * len(ws) + list(extra_specs) + side_specs,
        out_specs=[o_spec] + side_specs,
        out_shape=[jax.ShapeDtypeStruct((m, n), out_dtype)] + side_shapes,
        scratch_shapes=[pltpu.VMEM((2, k, tn), BF16) for _ in ws],
        compiler_params=_params(2),
        name=name,
    )(*xs, *ws, *extra, *side_in)
    return outs


def _store_epilogue(accs, extra, o_ref):
    o_ref[...] = accs[0].astype(o_ref.dtype)


def _swiglu_epilogue(accs, extra, o_ref):
    g, u = accs
    o_ref[...] = (jax.nn.silu(g) * u).astype(o_ref.dtype)


def _rope(t, cos, sin_signed):
    return t * cos + pltpu.roll(t, HEAD_DIM // 2, 1) * sin_signed


def _ffn(h, w_gate, w_up, w_down, side_casts=()):
    a, *copies = _staged_matmul([h], [w_gate, w_up], n=w_gate.shape[1], out_dtype=BF16,
                                epilogue=_swiglu_epilogue, name="ffn_gate_up", tm=1024, tn=512,
                                side_casts=side_casts)
    y, = _staged_matmul([a], [w_down], n=w_down.shape[1], out_dtype=F32,
                        epilogue=_store_epilogue, name="ffn_down", tm=512, tn=512)
    return y, copies


def _in_proj_pool(h, w, tm=1024, tn=512):
    return _staged_matmul([h], [w], n=POOL_WIDTH, out_dtype=F32, epilogue=_store_epilogue,
                          name="mix_in_proj_pool", tm=tm, tn=tn, col0=ATTN_WIDTH + 2 * KV_WIDTH)[0]


def _resident(shape):
    return pl.BlockSpec(shape, lambda t: (0,) * len(shape), pipeline_mode=pl.Buffered(1))


def _residual_qkv_kernel(x_ref, y_ref, gpost_ref, gnext_ref, w_ref, cos_ref, sin_ref,
                         xo_ref, h_ref, qkv_ref, *, weight):
    xn = x_ref[...] + weight * _rms(y_ref[...], gpost_ref[...])
    xo_ref[...] = xn
    h = _rms(xn, gnext_ref[...]).astype(BF16)
    h_ref[...] = h
    acc = jnp.dot(h, w_ref[...], preferred_element_type=F32)
    for c in range(acc.shape[1] // HEAD_DIM):
        cols = slice(c * HEAD_DIM, (c + 1) * HEAD_DIM)
        if c * HEAD_DIM < ATTN_WIDTH + KV_WIDTH:
            kind = 0 if c * HEAD_DIM < ATTN_WIDTH else 1
            val = _rope(acc[:, cols], cos_ref[kind], sin_ref[kind])
        else:
            val = acc[:, cols]
        qkv_ref[:, cols] = val.astype(qkv_ref.dtype)


def _residual_qkv(x, y, g_post, g_next, w_qkv, cos2, sin2, seq, weight, tm=128):
    m, d = x.shape
    n = ATTN_WIDTH + 2 * KV_WIDTH
    assert w_qkv.shape == (d, n) and w_qkv.dtype == BF16 and m % tm == 0 and seq % tm == 0
    row = lambda width: pl.BlockSpec((tm, width), lambda t: (t, 0))
    vec = pl.BlockSpec((1, d), lambda t: (0, 0))
    table = pl.BlockSpec((2, tm, HEAD_DIM), lambda t: (0, t % (seq // tm), 0))
    return pl.pallas_call(
        functools.partial(_residual_qkv_kernel, weight=weight),
        grid=(m // tm,),
        in_specs=[row(d), row(d), vec, vec, _resident((d, n)), table, table],
        out_specs=[row(d), row(d), row(n)],
        out_shape=[jax.ShapeDtypeStruct((m, d), F32), jax.ShapeDtypeStruct((m, d), BF16),
                   jax.ShapeDtypeStruct((m, n), BF16)],
        compiler_params=_params(1),
        name="residual_qkv",
    )(x, y, g_post, g_next, w_qkv, cos2, sin2)


def _out_proj_residual_kernel(a_ref, p_ref, x_ref, gpost_ref, gnext_ref, w_ref, xo_ref, h_ref):
    ka = a_ref.shape[1]
    y = (jnp.dot(a_ref[...], w_ref[0:ka, :], preferred_element_type=F32)
         + jnp.dot(p_ref[...], w_ref[ka:, :], preferred_element_type=F32))
    xn = x_ref[...] + _rms(y, gpost_ref[...])
    xo_ref[...] = xn
    h_ref[...] = _rms(xn, gnext_ref[...]).astype(h_ref.dtype)


def _out_proj_residual(a, pm, x, g_post, g_next, w_bf16, tm=128):
    m, d = x.shape
    k = a.shape[1] + pm.shape[1]
    assert w_bf16.shape == (k, d) and w_bf16.dtype == BF16 and m % tm == 0
    row = lambda width: pl.BlockSpec((tm, width), lambda t: (t, 0))
    vec = pl.BlockSpec((1, d), lambda t: (0, 0))
    return pl.pallas_call(
        _out_proj_residual_kernel,
        grid=(m // tm,),
        in_specs=[row(a.shape[1]), row(pm.shape[1]), row(d), vec, vec, _resident((k, d))],
        out_specs=[row(d), row(d)],
        out_shape=[jax.ShapeDtypeStruct((m, d), F32), jax.ShapeDtypeStruct((m, d), BF16)],
        compiler_params=_params(1),
        name="out_proj_residual",
    )(a, pm, x, g_post, g_next, w_bf16)


def _attn_kernel(sinks_ref, q_ref, k_ref, v_ref, o_ref, bias_s, *, seq):
    kvh = pl.program_id(1)
    g, blk = GQA_GROUP, BLOCK
    key = lax.broadcasted_iota(jnp.int32, (2 * blk, blk), 0)
    qry = lax.broadcasted_iota(jnp.int32, (2 * blk, blk), 1)
    cur_ok = (key >= blk) & (key - blk <= qry)
    prev_ok = (key < blk) & (key > qry)
    bias_s[0] = jnp.where(cur_ok, 0.0, -jnp.inf)
    bias_s[1] = jnp.where(cur_ok | prev_ok, 0.0, -jnp.inf)
    sinks = [jnp.full((1, blk), sinks_ref[kvh * g + c], F32) for c in range(g)]
    contract_last = (((1,), (1,)), ((), ()))
    contract_first = (((0,), (0,)), ((), ()))

    def body(n, carry):
        r0 = pl.multiple_of(n * blk, blk)
        rp = pl.multiple_of(jnp.maximum(n - 1, 0) * blk, blk)
        kw = jnp.concatenate([k_ref[pl.ds(rp, blk), :], k_ref[pl.ds(r0, blk), :]], axis=0)
        vw = jnp.concatenate([v_ref[pl.ds(rp, blk), :], v_ref[pl.ds(r0, blk), :]], axis=0)
        bias = bias_s[jnp.minimum(n, 1)]
        for c in range(g):
            qc = q_ref[pl.ds(r0, blk), c * HEAD_DIM:(c + 1) * HEAD_DIM]
            s = lax.dot_general(kw, qc, contract_last, preferred_element_type=F32) + bias
            m = jnp.maximum(jnp.max(s, axis=0, keepdims=True), sinks[c])
            e = jnp.exp(s - m)
            denom = jnp.sum(e, axis=0, keepdims=True) + jnp.exp(sinks[c] - m)
            p = (e * (1.0 / denom)).astype(BF16)
            o = lax.dot_general(p, vw, contract_first, preferred_element_type=F32)
            o_ref[pl.ds(r0, blk), c * HEAD_DIM:(c + 1) * HEAD_DIM] = o.astype(o_ref.dtype)
        return carry

    lax.fori_loop(0, seq // blk, body, 0, unroll=16)


def _attention(qkv, sinks, batch, seq):
    m = qkv.shape[0]
    qw = GQA_GROUP * HEAD_DIM
    k_blk0 = ATTN_WIDTH // HEAD_DIM
    v_blk0 = (ATTN_WIDTH + KV_WIDTH) // HEAD_DIM
    return pl.pallas_call(
        functools.partial(_attn_kernel, seq=seq),
        grid=(batch, N_KV_HEADS),
        in_specs=[pl.BlockSpec(memory_space=pltpu.SMEM),
                  pl.BlockSpec((seq, qw), lambda b, h: (b, h)),
                  pl.BlockSpec((seq, HEAD_DIM), lambda b, h: (b, k_blk0 + h)),
                  pl.BlockSpec((seq, HEAD_DIM), lambda b, h: (b, v_blk0 + h))],
        out_specs=pl.BlockSpec((seq, qw), lambda b, h: (b, h)),
        out_shape=jax.ShapeDtypeStruct((m, ATTN_WIDTH), BF16),
        scratch_shapes=[pltpu.VMEM((2, 2 * BLOCK, BLOCK), F32)],
        compiler_params=_params(2),
        name="swa_attention",
    )(sinks, qkv, qkv, qkv)


def _pool_kernel(p_ref, w_ref, sc_ref, o_ref, pad_s, y_s, *, seq, rows):
    grp = pl.program_id(1)
    gw = POOL_GROUP_WIDTH
    pad_s[0:POOL_HALO, :] = jnp.zeros((POOL_HALO, gw), F32)
    pad_s[POOL_HALO:, :] = p_ref[...]
    ext = rows + POOL_HALO
    lrow = lax.broadcasted_iota(jnp.int32, (rows, 128), 0)

    for gi, w in enumerate(POOL_WINDOWS):
        assert w & (w - 1) == 0 and w - 1 <= POOL_HALO

        @pl.when(grp == gi)
        def _(w=w):
            def body(r, carry):
                r0 = pl.multiple_of(r * rows, rows)
                cnt = jnp.minimum(lrow + (r0 + 1), w).astype(F32)
                for c in range(gw // 128):
                    x = pad_s[pl.ds(r0, ext), c * 128:(c + 1) * 128]
                    s = x
                    d = 1
                    while d < w:
                        s = s + pltpu.roll(s, d, 0)
                        d *= 2
                    y = s[POOL_HALO:] / cnt - x[POOL_HALO:]
                    y_s[pl.ds(r0, rows), c * 128:(c + 1) * 128] = y.astype(BF16)
                return carry

            lax.fori_loop(0, seq // rows, body, 0)

    out = jnp.dot(y_s[...], w_ref[0].astype(BF16), preferred_element_type=F32) * sc_ref[...]
    o_ref[...] = out.astype(o_ref.dtype)


def _pool(proj, pool_w, pool_scale, batch, seq, rows=128):
    m = proj.shape[0]
    gw = POOL_GROUP_WIDTH
    ng = len(POOL_WINDOWS)
    return pl.pallas_call(
        functools.partial(_pool_kernel, seq=seq, rows=rows),
        grid=(batch, ng),
        in_specs=[pl.BlockSpec((seq, gw), lambda b, g: (b, g)),
                  pl.BlockSpec((1, gw, gw), lambda b, g: (g, 0, 0)),
                  pl.BlockSpec((1, gw), lambda b, g: (0, g))],
        out_specs=pl.BlockSpec((seq, gw), lambda b, g: (b, g)),
        out_shape=jax.ShapeDtypeStruct((m, POOL_WIDTH), BF16),
        scratch_shapes=[pltpu.VMEM((seq + POOL_HALO, gw), F32), pltpu.VMEM((seq, gw), BF16)],
        compiler_params=_params(2),
        name="multiscale_pool",
    )(proj, pool_w, pool_scale)


def _rope_tables(seq):
    pos = jnp.arange(seq, dtype=F32)
    inv_freq = ROPE_THETA ** (-jnp.arange(0, HEAD_DIM, 2, dtype=F32) / HEAD_DIM)
    ang = pos[:, None] * inv_freq[None, :]
    cos, sin = jnp.cos(ang), jnp.sin(ang)
    cos = jnp.concatenate([cos, cos], axis=-1)
    sin = jnp.concatenate([-sin, sin], axis=-1)
    scale = HEAD_DIM ** -0.5
    return jnp.stack([cos * scale, cos]), jnp.stack([sin * scale, sin])


def kernel(x, ffn1_pre_g, ffn1_w_gate, ffn1_w_up, ffn1_w_down, ffn1_post_g, mix_pre_g, w_in, attn_sinks, pool_w, pool_scale, w_out, mix_post_g, ffn2_pre_g, ffn2_w_gate, ffn2_w_up, ffn2_w_down, ffn2_post_g):
    b, s, d = x.shape
    depth = ffn1_pre_g.shape[0]
    cos2, sin2 = _rope_tables(s)
    vec = lambda g: g.reshape(1, -1)
    xf = x.reshape(b * s, d)
    h = _prenorm(xf, vec(ffn1_pre_g[0]))
    for l in range(depth):
        side = [(w_in[l], SIDE_CAST_ROWS, ATTN_WIDTH + 2 * KV_WIDTH),
                (w_out[l], SIDE_CAST_ROWS, d)]
        y, (w_qkv, w_o) = _ffn(h, ffn1_w_gate[l], ffn1_w_up[l], ffn1_w_down[l], side)
        xf, h, qkv = _residual_qkv(xf, y, vec(ffn1_post_g[l]), vec(mix_pre_g[l]), w_qkv,
                                   cos2, sin2, s, FFN_RES_WEIGHT)
        a = _attention(qkv, attn_sinks[l], b, s)
        pm = _pool(_in_proj_pool(h, w_in[l]), pool_w[l], vec(pool_scale[l]), b, s)
        xf, h = _out_proj_residual(a, pm, xf, vec(mix_post_g[l]), vec(ffn2_pre_g[l]), w_o)

        y, _ = _ffn(h, ffn2_w_gate[l], ffn2_w_up[l], ffn2_w_down[l])
        g_next = vec(ffn1_pre_g[l + 1]) if l + 1 < depth else None
        xf, h = _residual(xf, y, vec(ffn2_post_g[l]), g_next, FFN_RES_WEIGHT)
    return xf.reshape(b, s, d)
```

```python
import functools

import jax
import jax.numpy as jnp
from jax import lax
from jax.experimental import pallas as pl
from jax.experimental.pallas import tpu as pltpu

D_MODEL = 4096
HEAD_DIM = 128
N_HEADS = 16
N_KV_HEADS = 4
GQA_GROUP = N_HEADS // N_KV_HEADS
ATTN_WIDTH = N_HEADS * HEAD_DIM
KV_WIDTH = N_KV_HEADS * HEAD_DIM
BLOCK = 128
ROPE_THETA = 10000.0
POOL_WIDTH = D_MODEL - ATTN_WIDTH
POOL_WINDOWS = (2, 4, 8, 16)
POOL_GROUP_WIDTH = POOL_WIDTH // len(POOL_WINDOWS)
IN_PROJ_WIDTH = ATTN_WIDTH + 2 * KV_WIDTH + POOL_WIDTH
FFN_RES_WEIGHT = 0.5
RMS_EPS = 1e-6
POOL_HALO = 16

BF16 = jnp.bfloat16
F32 = jnp.float32
BF16_SUBLANES = 16
SIDE_CAST_ROWS = 32

VMEM_LIMIT_BYTES = 56 * 1024 * 1024


def _params(n_axes):
    return pltpu.CompilerParams(
        dimension_semantics=("arbitrary",) * n_axes,
        vmem_limit_bytes=VMEM_LIMIT_BYTES,
    )


def _rms(x, g):
    ms = jnp.mean(x * x, axis=-1, keepdims=True)
    return x * lax.rsqrt(ms + RMS_EPS) * g


def _norm_kernel(x_ref, g_ref, h_ref):
    h_ref[...] = _rms(x_ref[...], g_ref[...]).astype(h_ref.dtype)


def _prenorm(x, g, tm=512):
    m, d = x.shape
    return pl.pallas_call(
        _norm_kernel,
        grid=(m // tm,),
        in_specs=[pl.BlockSpec((tm, d), lambda i: (i, 0)),
                  pl.BlockSpec((1, d), lambda i: (0, 0))],
        out_specs=pl.BlockSpec((tm, d), lambda i: (i, 0)),
        out_shape=jax.ShapeDtypeStruct((m, d), BF16),
        compiler_params=_params(1),
        name="prenorm",
    )(x, g)


def _residual_kernel(x_ref, y_ref, gpost_ref, *rest, weight, with_next):
    if with_next:
        gnext_ref, xo_ref, h_ref = rest
    else:
        (xo_ref,) = rest
    r = _rms(y_ref[...], gpost_ref[...])
    if weight != 1.0:
        r = weight * r
    xn = x_ref[...] + r
    xo_ref[...] = xn
    if with_next:
        h_ref[...] = _rms(xn, gnext_ref[...]).astype(h_ref.dtype)


def _residual(x, y, g_post, g_next, weight, tm=256):
    m, d = x.shape
    with_next = g_next is not None
    row = pl.BlockSpec((tm, d), lambda i: (i, 0))
    vec = pl.BlockSpec((1, d), lambda i: (0, 0))
    in_specs = [row, row, vec] + ([vec] if with_next else [])
    args = (x, y, g_post) + ((g_next,) if with_next else ())
    out_specs = [row] + ([row] if with_next else [])
    out_shape = [jax.ShapeDtypeStruct((m, d), F32)] + (
        [jax.ShapeDtypeStruct((m, d), BF16)] if with_next else [])
    outs = pl.pallas_call(
        functools.partial(_residual_kernel, weight=weight, with_next=with_next),
        grid=(m // tm,),
        in_specs=in_specs,
        out_specs=out_specs,
        out_shape=out_shape,
        compiler_params=_params(1),
        name="residual_norm",
    )(*args)
    return (outs[0], outs[1]) if with_next else (outs[0], None)


def _staged_kernel(*refs, n_x, n_w, n_alias, side_steps, nj, slab, epilogue):
    n_side = len(side_steps)
    n_in = n_x + n_w
    x_refs = refs[:n_x]
    w_refs = refs[n_x:n_in]
    side_in = refs[n_in:n_in + n_side]
    n_in += n_side + n_alias
    o_ref = refs[n_in]
    side_out = refs[n_in + 1:n_in + 1 + n_side]
    w_s = refs[n_in + 1 + n_side:]
    j = pl.program_id(0)
    i = pl.program_id(1)

    for src, dst, steps in zip(side_in, side_out, side_steps):
        @pl.when(j * pl.num_programs(1) + i < steps)
        def _(src=src, dst=dst):
            dst[...] = src[...].astype(dst.dtype)

    @pl.when(j < nj)
    def _():
        rows = pl.ds(pl.multiple_of(i * slab, slab), slab)
        for w_ref, s in zip(w_refs, w_s):
            s[j % 2, rows, :] = w_ref[...].astype(BF16)

    @pl.when(j > 0)
    def _():
        slot = (j + 1) % 2
        accs = []
        for s in w_s:
            acc, k0 = None, 0
            for x_ref in x_refs:
                kx = x_ref.shape[1]
                part = jnp.dot(x_ref[...], s[slot, k0:k0 + kx, :], preferred_element_type=F32)
                acc = part if acc is None else acc + part
                k0 += kx
            accs.append(acc)
        epilogue(accs, o_ref)


def _staged_matmul(xs, ws, *, n, out_dtype, epilogue, name, tm, tn, col0=0, out_cols=None,
                   into=None, side_casts=()):
    m = xs[0].shape[0]
    k = sum(x.shape[1] for x in xs)
    assert all(w.shape[0] == k for w in ws) and m % tm == 0 and col0 % tn == 0 and n % tn == 0
    ni, nj = m // tm, n // tn
    assert k % (ni * BF16_SUBLANES) == 0
    slab = k // ni
    cb = col0 // tn
    ob = cb if (out_cols or into is not None) else 0
    out_struct = jax.ShapeDtypeStruct((m, out_cols or n), out_dtype) if into is None else (
        jax.ShapeDtypeStruct(into.shape, into.dtype))
    side_in, side_specs, side_shapes, side_steps = [], [], [], []
    for arr, rows, cols in side_casts:
        steps = arr.shape[0] // rows
        assert arr.shape[0] % rows == 0 and rows % BF16_SUBLANES == 0 and steps <= (nj + 1) * ni
        side_in.append(arr)
        side_specs.append(pl.BlockSpec(
            (rows, cols), lambda j, i, steps=steps: (jnp.minimum(j * ni + i, steps - 1), 0)))
        side_shapes.append(jax.ShapeDtypeStruct((arr.shape[0], cols), BF16))
        side_steps.append(steps)
    row_tile = lambda j, i: jnp.where(j == 0, 0, i)
    x_specs = [pl.BlockSpec((tm, x.shape[1]), lambda j, i: (row_tile(j, i), 0)) for x in xs]
    w_spec = pl.BlockSpec((slab, tn), lambda j, i: (jnp.where(j < nj, i, ni - 1),
                                                    cb + jnp.minimum(j, nj - 1)))
    o_spec = pl.BlockSpec((tm, tn), lambda j, i: (row_tile(j, i), ob + jnp.maximum(j - 1, 0)))
    alias_in = [] if into is None else [into]
    n_in = len(xs) + len(ws) + len(side_in)
    kern = functools.partial(_staged_kernel, n_x=len(xs), n_w=len(ws), n_alias=len(alias_in),
                             side_steps=tuple(side_steps), nj=nj, slab=slab, epilogue=epilogue)
    outs = pl.pallas_call(
        kern,
        grid=(nj + 1, ni),
        in_specs=(x_specs + [w_spec] * len(ws) + side_specs
                  + [pl.BlockSpec(memory_space=pl.ANY)] * len(alias_in)),
        out_specs=[o_spec] + side_specs,
        out_shape=[out_struct] + side_shapes,
        scratch_shapes=[pltpu.VMEM((2, k, tn), BF16) for _ in ws],
        input_output_aliases={n_in: 0} if alias_in else {},
        compiler_params=_params(2),
        name=name,
    )(*xs, *ws, *side_in, *alias_in)
    return outs


def _store_epilogue(accs, o_ref):
    o_ref[...] = accs[0].astype(o_ref.dtype)


def _swiglu_epilogue(accs, o_ref):
    g, u = accs
    o_ref[...] = (jax.nn.silu(g) * u).astype(o_ref.dtype)


def _rope(t, cos, sin_signed):
    return t * cos + pltpu.roll(t, HEAD_DIM // 2, 1) * sin_signed


def _ffn(h, w_gate, w_up, w_down, side_casts=()):
    d_ff = w_gate.shape[1]
    tn, tn_tail = 512, 256
    n_main = d_ff // tn * tn
    a, *copies = _staged_matmul([h], [w_gate, w_up], n=n_main, out_cols=d_ff, out_dtype=BF16,
                                epilogue=_swiglu_epilogue, name="ffn_gate_up", tm=1024, tn=tn,
                                side_casts=side_casts)
    if n_main < d_ff:
        a, = _staged_matmul([h], [w_gate, w_up], n=d_ff - n_main, col0=n_main, into=a,
                            out_dtype=BF16, epilogue=_swiglu_epilogue, name="ffn_gate_up_tail",
                            tm=1024, tn=tn_tail)
    y, = _staged_matmul([a], [w_down], n=w_down.shape[1], out_dtype=F32,
                        epilogue=_store_epilogue, name="ffn_down", tm=512, tn=512)
    return y, copies


def _in_proj_pool(h, w, tm=1024, tn=512):
    return _staged_matmul([h], [w], n=POOL_WIDTH, out_dtype=F32, epilogue=_store_epilogue,
                          name="mix_in_proj_pool", tm=tm, tn=tn, col0=ATTN_WIDTH + 2 * KV_WIDTH)[0]


def _resident(shape):
    return pl.BlockSpec(shape, lambda t: (0,) * len(shape), pipeline_mode=pl.Buffered(1))


def _residual_qkv_kernel(x_ref, y_ref, gpost_ref, gnext_ref, w_ref, cos_ref, sin_ref,
                         xo_ref, h_ref, qkv_ref, *, weight):
    xn = x_ref[...] + weight * _rms(y_ref[...], gpost_ref[...])
    xo_ref[...] = xn
    h = _rms(xn, gnext_ref[...]).astype(BF16)
    h_ref[...] = h
    acc = jnp.dot(h, w_ref[...], preferred_element_type=F32)
    for c in range(acc.shape[1] // HEAD_DIM):
        cols = slice(c * HEAD_DIM, (c + 1) * HEAD_DIM)
        if c * HEAD_DIM < ATTN_WIDTH + KV_WIDTH:
            kind = 0 if c * HEAD_DIM < ATTN_WIDTH else 1
            val = _rope(acc[:, cols], cos_ref[kind], sin_ref[kind])
        else:
            val = acc[:, cols]
        qkv_ref[:, cols] = val.astype(qkv_ref.dtype)


def _residual_qkv(x, y, g_post, g_next, w_qkv, cos2, sin2, seq, weight, tm=128):
    m, d = x.shape
    n = ATTN_WIDTH + 2 * KV_WIDTH
    assert w_qkv.shape == (d, n) and w_qkv.dtype == BF16 and m % tm == 0 and seq % tm == 0
    row = lambda width: pl.BlockSpec((tm, width), lambda t: (t, 0))
    vec = pl.BlockSpec((1, d), lambda t: (0, 0))
    table = pl.BlockSpec((2, tm, HEAD_DIM), lambda t: (0, t % (seq // tm), 0))
    return pl.pallas_call(
        functools.partial(_residual_qkv_kernel, weight=weight),
        grid=(m // tm,),
        in_specs=[row(d), row(d), vec, vec, _resident((d, n)), table, table],
        out_specs=[row(d), row(d), row(n)],
        out_shape=[jax.ShapeDtypeStruct((m, d), F32), jax.ShapeDtypeStruct((m, d), BF16),
                   jax.ShapeDtypeStruct((m, n), BF16)],
        compiler_params=_params(1),
        name="residual_qkv",
    )(x, y, g_post, g_next, w_qkv, cos2, sin2)


def _out_proj_residual_kernel(a_ref, p_ref, x_ref, gpost_ref, gnext_ref, w_ref, xo_ref, h_ref):
    ka = a_ref.shape[1]
    y = (jnp.dot(a_ref[...], w_ref[0:ka, :], preferred_element_type=F32)
         + jnp.dot(p_ref[...], w_ref[ka:, :], preferred_element_type=F32))
    xn = x_ref[...] + _rms(y, gpost_ref[...])
    xo_ref[...] = xn
    h_ref[...] = _rms(xn, gnext_ref[...]).astype(h_ref.dtype)


def _out_proj_residual(a, pm, x, g_post, g_next, w_bf16, tm=128):
    m, d = x.shape
    k = a.shape[1] + pm.shape[1]
    assert w_bf16.shape == (k, d) and w_bf16.dtype == BF16 and m % tm == 0
    row = lambda width: pl.BlockSpec((tm, width), lambda t: (t, 0))
    vec = pl.BlockSpec((1, d), lambda t: (0, 0))
    return pl.pallas_call(
        _out_proj_residual_kernel,
        grid=(m // tm,),
        in_specs=[row(a.shape[1]), row(pm.shape[1]), row(d), vec, vec, _resident((k, d))],
        out_specs=[row(d), row(d)],
        out_shape=[jax.ShapeDtypeStruct((m, d), F32), jax.ShapeDtypeStruct((m, d), BF16)],
        compiler_params=_params(1),
        name="out_proj_residual",
    )(a, pm, x, g_post, g_next, w_bf16)


def _attn_kernel(sinks_ref, q_ref, k_ref, v_ref, o_ref, bias_s, *, seq):
    kvh = pl.program_id(1)
    g, blk = GQA_GROUP, BLOCK
    key = lax.broadcasted_iota(jnp.int32, (2 * blk, blk), 0)
    qry = lax.broadcasted_iota(jnp.int32, (2 * blk, blk), 1)
    cur_ok = (key >= blk) & (key - blk <= qry)
    prev_ok = (key < blk) & (key > qry)
    bias_s[0] = jnp.where(cur_ok, 0.0, -jnp.inf)
    bias_s[1] = jnp.where(cur_ok | prev_ok, 0.0, -jnp.inf)
    sinks = [jnp.full((1, blk), sinks_ref[kvh * g + c], F32) for c in range(g)]
    contract_last = (((1,), (1,)), ((), ()))
    contract_first = (((0,), (0,)), ((), ()))

    def body(n, carry):
        r0 = pl.multiple_of(n * blk, blk)
        rp = pl.multiple_of(jnp.maximum(n - 1, 0) * blk, blk)
        kw = jnp.concatenate([k_ref[pl.ds(rp, blk), :], k_ref[pl.ds(r0, blk), :]], axis=0)
        vw = jnp.concatenate([v_ref[pl.ds(rp, blk), :], v_ref[pl.ds(r0, blk), :]], axis=0)
        bias = bias_s[jnp.minimum(n, 1)]
        for c in range(g):
            qc = q_ref[pl.ds(r0, blk), c * HEAD_DIM:(c + 1) * HEAD_DIM]
            s = lax.dot_general(kw, qc, contract_last, preferred_element_type=F32) + bias
            m = jnp.maximum(jnp.max(s, axis=0, keepdims=True), sinks[c])
            e = jnp.exp(s - m)
            denom = jnp.sum(e, axis=0, keepdims=True) + jnp.exp(sinks[c] - m)
            p = (e * (1.0 / denom)).astype(BF16)
            o = lax.dot_general(p, vw, contract_first, preferred_element_type=F32)
            o_ref[pl.ds(r0, blk), c * HEAD_DIM:(c + 1) * HEAD_DIM] = o.astype(o_ref.dtype)
        return carry

    lax.fori_loop(0, seq // blk, body, 0, unroll=16)


def _attention(qkv, sinks, batch, seq):
    m = qkv.shape[0]
    qw = GQA_GROUP * HEAD_DIM
    k_blk0 = ATTN_WIDTH // HEAD_DIM
    v_blk0 = (ATTN_WIDTH + KV_WIDTH) // HEAD_DIM
    return pl.pallas_call(
        functools.partial(_attn_kernel, seq=seq),
        grid=(batch, N_KV_HEADS),
        in_specs=[pl.BlockSpec(memory_space=pltpu.SMEM),
                  pl.BlockSpec((seq, qw), lambda b, h: (b, h)),
                  pl.BlockSpec((seq, HEAD_DIM), lambda b, h: (b, k_blk0 + h)),
                  pl.BlockSpec((seq, HEAD_DIM), lambda b, h: (b, v_blk0 + h))],
        out_specs=pl.BlockSpec((seq, qw), lambda b, h: (b, h)),
        out_shape=jax.ShapeDtypeStruct((m, ATTN_WIDTH), BF16),
        scratch_shapes=[pltpu.VMEM((2, 2 * BLOCK, BLOCK), F32)],
        compiler_params=_params(2),
        name="swa_attention",
    )(sinks, qkv, qkv, qkv)


def _pool_kernel(p_ref, w_ref, sc_ref, o_ref, pad_s, y_s, *, seq, rows):
    grp = pl.program_id(1)
    gw = POOL_GROUP_WIDTH
    pad_s[0:POOL_HALO, :] = jnp.zeros((POOL_HALO, gw), F32)
    pad_s[POOL_HALO:, :] = p_ref[...]
    ext = rows + POOL_HALO
    lrow = lax.broadcasted_iota(jnp.int32, (rows, 128), 0)

    for gi, w in enumerate(POOL_WINDOWS):
        assert w & (w - 1) == 0 and w - 1 <= POOL_HALO

        @pl.when(grp == gi)
        def _(w=w):
            def body(r, carry):
                r0 = pl.multiple_of(r * rows, rows)
                cnt = jnp.minimum(lrow + (r0 + 1), w).astype(F32)
                for c in range(gw // 128):
                    x = pad_s[pl.ds(r0, ext), c * 128:(c + 1) * 128]
                    s = x
                    d = 1
                    while d < w:
                        s = s + pltpu.roll(s, d, 0)
                        d *= 2
                    y = s[POOL_HALO:] / cnt - x[POOL_HALO:]
                    y_s[pl.ds(r0, rows), c * 128:(c + 1) * 128] = y.astype(BF16)
                return carry

            lax.fori_loop(0, seq // rows, body, 0)

    out = jnp.dot(y_s[...], w_ref[0].astype(BF16), preferred_element_type=F32) * sc_ref[...]
    o_ref[...] = out.astype(o_ref.dtype)


def _pool(proj, pool_w, pool_scale, batch, seq, rows=128):
    m = proj.shape[0]
    gw = POOL_GROUP_WIDTH
    ng = len(POOL_WINDOWS)
    return pl.pallas_call(
        functools.partial(_pool_kernel, seq=seq, rows=rows),
        grid=(batch, ng),
        in_specs=[pl.BlockSpec((seq, gw), lambda b, g: (b, g)),
                  pl.BlockSpec((1, gw, gw), lambda b, g: (g, 0, 0)),
                  pl.BlockSpec((1, gw), lambda b, g: (0, g))],
        out_specs=pl.BlockSpec((seq, gw), lambda b, g: (b, g)),
        out_shape=jax.ShapeDtypeStruct((m, POOL_WIDTH), BF16),
        scratch_shapes=[pltpu.VMEM((seq + POOL_HALO, gw), F32), pltpu.VMEM((seq, gw), BF16)],
        compiler_params=_params(2),
        name="multiscale_pool",
    )(proj, pool_w, pool_scale)


def _rope_tables(seq):
    pos = jnp.arange(seq, dtype=F32)
    inv_freq = ROPE_THETA ** (-jnp.arange(0, HEAD_DIM, 2, dtype=F32) / HEAD_DIM)
    ang = pos[:, None] * inv_freq[None, :]
    cos, sin = jnp.cos(ang), jnp.sin(ang)
    cos = jnp.concatenate([cos, cos], axis=-1)
    sin = jnp.concatenate([-sin, sin], axis=-1)
    scale = HEAD_DIM ** -0.5
    return jnp.stack([cos * scale, cos]), jnp.stack([sin * scale, sin])


def kernel(x, ffn1_pre_g, ffn1_w_gate, ffn1_w_up, ffn1_w_down, ffn1_post_g, mix_pre_g, w_in, attn_sinks, pool_w, pool_scale, w_out, mix_post_g, ffn2_pre_g, ffn2_w_gate, ffn2_w_up, ffn2_w_down, ffn2_post_g):
    b, s, d = x.shape
    depth = ffn1_pre_g.shape[0]
    cos2, sin2 = _rope_tables(s)
    vec = lambda g: g.reshape(1, -1)
    xf = x.reshape(b * s, d)
    h = _prenorm(xf, vec(ffn1_pre_g[0]))
    for l in range(depth):
        side = [(w_in[l], SIDE_CAST_ROWS, ATTN_WIDTH + 2 * KV_WIDTH),
                (w_out[l], SIDE_CAST_ROWS, d)]
        y, (w_qkv, w_o) = _ffn(h, ffn1_w_gate[l], ffn1_w_up[l], ffn1_w_down[l], side)
        xf, h, qkv = _residual_qkv(xf, y, vec(ffn1_post_g[l]), vec(mix_pre_g[l]), w_qkv,
                                   cos2, sin2, s, FFN_RES_WEIGHT)
        a = _attention(qkv, attn_sinks[l], b, s)
        pm = _pool(_in_proj_pool(h, w_in[l]), pool_w[l], vec(pool_scale[l]), b, s)
        xf, h = _out_proj_residual(a, pm, xf, vec(mix_post_g[l]), vec(ffn2_pre_g[l]), w_o)

        y, _ = _ffn(h, ffn2_w_gate[l], ffn2_w_up[l], ffn2_w_down[l])
        g_next = vec(ffn1_pre_g[l + 1]) if l + 1 < depth else None
        xf, h = _residual(xf, y, vec(ffn2_post_g[l]), g_next, FFN_RES_WEIGHT)
    return xf.reshape(b, s, d)
```

```python
import functools

import jax
import jax.numpy as jnp
import numpy as np
from jax import lax
from jax.experimental import pallas as pl
from jax.experimental.pallas import tpu as pltpu

D_MODEL = 4096
HEAD_DIM = 128
N_HEADS = 16
N_KV_HEADS = 4
GQA_GROUP = N_HEADS // N_KV_HEADS
ATTN_WIDTH = N_HEADS * HEAD_DIM
KV_WIDTH = N_KV_HEADS * HEAD_DIM
BLOCK = 128
ROPE_THETA = 10000.0
POOL_WIDTH = D_MODEL - ATTN_WIDTH
POOL_WINDOWS = (2, 4, 8, 16)
POOL_GROUP_WIDTH = POOL_WIDTH // len(POOL_WINDOWS)
IN_PROJ_WIDTH = ATTN_WIDTH + 2 * KV_WIDTH + POOL_WIDTH
FFN_RES_WEIGHT = 0.5
RMS_EPS = 1e-6
POOL_HALO = 16

BF16 = jnp.bfloat16
F32 = jnp.float32
BF16_SUBLANES = 16
SIDE_CAST_ROWS = 32

VMEM_LIMIT_BYTES = 56 * 1024 * 1024


def _params(n_axes):
    return pltpu.CompilerParams(
        dimension_semantics=("arbitrary",) * n_axes,
        vmem_limit_bytes=VMEM_LIMIT_BYTES,
    )


def _rms(x, g):
    ms = jnp.mean(x * x, axis=-1, keepdims=True)
    return x * lax.rsqrt(ms + RMS_EPS) * g


def _norm_kernel(x_ref, g_ref, h_ref):
    h_ref[...] = _rms(x_ref[...], g_ref[...]).astype(h_ref.dtype)


def _prenorm(x, g, tm=512):
    m, d = x.shape
    return pl.pallas_call(
        _norm_kernel,
        grid=(m // tm,),
        in_specs=[pl.BlockSpec((tm, d), lambda i: (i, 0)),
                  pl.BlockSpec((1, d), lambda i: (0, 0))],
        out_specs=pl.BlockSpec((tm, d), lambda i: (i, 0)),
        out_shape=jax.ShapeDtypeStruct((m, d), BF16),
        compiler_params=_params(1),
        name="prenorm",
    )(x, g)


def _residual_kernel(x_ref, y_ref, gpost_ref, *rest, weight, with_next):
    if with_next:
        gnext_ref, xo_ref, h_ref = rest
    else:
        (xo_ref,) = rest
    r = _rms(y_ref[...], gpost_ref[...])
    if weight != 1.0:
        r = weight * r
    xn = x_ref[...] + r
    xo_ref[...] = xn
    if with_next:
        h_ref[...] = _rms(xn, gnext_ref[...]).astype(h_ref.dtype)


def _residual(x, y, g_post, g_next, weight, tm=256):
    m, d = x.shape
    with_next = g_next is not None
    row = pl.BlockSpec((tm, d), lambda i: (i, 0))
    vec = pl.BlockSpec((1, d), lambda i: (0, 0))
    in_specs = [row, row, vec] + ([vec] if with_next else [])
    args = (x, y, g_post) + ((g_next,) if with_next else ())
    out_specs = [row] + ([row] if with_next else [])
    out_shape = [jax.ShapeDtypeStruct((m, d), F32)] + (
        [jax.ShapeDtypeStruct((m, d), BF16)] if with_next else [])
    outs = pl.pallas_call(
        functools.partial(_residual_kernel, weight=weight, with_next=with_next),
        grid=(m // tm,),
        in_specs=in_specs,
        out_specs=out_specs,
        out_shape=out_shape,
        compiler_params=_params(1),
        name="residual_norm",
    )(*args)
    return (outs[0], outs[1]) if with_next else (outs[0], None)


def _staged_kernel(*refs, n_x, n_w, n_alias, side_steps, nj, slab, epilogue):
    n_side = len(side_steps)
    n_in = n_x + n_w
    x_refs = refs[:n_x]
    w_refs = refs[n_x:n_in]
    side_in = refs[n_in:n_in + n_side]
    n_in += n_side + n_alias
    o_ref = refs[n_in]
    side_out = refs[n_in + 1:n_in + 1 + n_side]
    w_s = refs[n_in + 1 + n_side:]
    j = pl.program_id(0)
    i = pl.program_id(1)

    for src, dst, steps in zip(side_in, side_out, side_steps):
        @pl.when(j * pl.num_programs(1) + i < steps)
        def _(src=src, dst=dst):
            dst[...] = src[...].astype(dst.dtype)

    @pl.when(j < nj)
    def _():
        rows = pl.ds(pl.multiple_of(i * slab, slab), slab)
        for w_ref, s in zip(w_refs, w_s):
            s[j % 2, rows, :] = w_ref[...].astype(BF16)

    @pl.when(j > 0)
    def _():
        slot = (j + 1) % 2
        accs = []
        for s in w_s:
            acc, k0 = None, 0
            for x_ref in x_refs:
                kx = x_ref.shape[1]
                part = jnp.dot(x_ref[...], s[slot, k0:k0 + kx, :], preferred_element_type=F32)
                acc = part if acc is None else acc + part
                k0 += kx
            accs.append(acc)
        epilogue(accs, o_ref)


def _staged_matmul(xs, ws, *, n, out_dtype, epilogue, name, tm, tn, col0=0, out_cols=None,
                   into=None, side_casts=()):
    m = xs[0].shape[0]
    k = sum(x.shape[1] for x in xs)
    assert all(w.shape[0] == k for w in ws) and m % tm == 0 and col0 % tn == 0 and n % tn == 0
    ni, nj = m // tm, n // tn
    assert k % (ni * BF16_SUBLANES) == 0
    slab = k // ni
    cb = col0 // tn
    ob = cb if (out_cols or into is not None) else 0
    out_struct = jax.ShapeDtypeStruct((m, out_cols or n), out_dtype) if into is None else (
        jax.ShapeDtypeStruct(into.shape, into.dtype))
    side_in, side_specs, side_shapes, side_steps = [], [], [], []
    for arr, rows, cols in side_casts:
        steps = arr.shape[0] // rows
        assert arr.shape[0] % rows == 0 and rows % BF16_SUBLANES == 0 and steps <= (nj + 1) * ni
        side_in.append(arr)
        side_specs.append(pl.BlockSpec(
            (rows, cols), lambda j, i, steps=steps: (jnp.minimum(j * ni + i, steps - 1), 0)))
        side_shapes.append(jax.ShapeDtypeStruct((arr.shape[0], cols), BF16))
        side_steps.append(steps)
    row_tile = lambda j, i: jnp.where(j == 0, 0, i)
    x_specs = [pl.BlockSpec((tm, x.shape[1]), lambda j, i: (row_tile(j, i), 0)) for x in xs]
    w_spec = pl.BlockSpec((slab, tn), lambda j, i: (jnp.where(j < nj, i, ni - 1),
                                                    cb + jnp.minimum(j, nj - 1)))
    o_spec = pl.BlockSpec((tm, tn), lambda j, i: (row_tile(j, i), ob + jnp.maximum(j - 1, 0)))
    alias_in = [] if into is None else [into]
    n_in = len(xs) + len(ws) + len(side_in)
    kern = functools.partial(_staged_kernel, n_x=len(xs), n_w=len(ws), n_alias=len(alias_in),
                             side_steps=tuple(side_steps), nj=nj, slab=slab, epilogue=epilogue)
    outs = pl.pallas_call(
        kern,
        grid=(nj + 1, ni),
        in_specs=(x_specs + [w_spec] * len(ws) + side_specs
                  + [pl.BlockSpec(memory_space=pl.ANY)] * len(alias_in)),
        out_specs=[o_spec] + side_specs,
        out_shape=[out_struct] + side_shapes,
        scratch_shapes=[pltpu.VMEM((2, k, tn), BF16) for _ in ws],
        input_output_aliases={n_in: 0} if alias_in else {},
        compiler_params=_params(2),
        name=name,
    )(*xs, *ws, *side_in, *alias_in)
    return outs


def _store_epilogue(accs, o_ref):
    o_ref[...] = accs[0].astype(o_ref.dtype)


def _swiglu_epilogue(accs, o_ref):
    g, u = accs
    o_ref[...] = (jax.nn.silu(g) * u).astype(o_ref.dtype)


def _rope(t, cos, sin_signed):
    return t * cos + pltpu.roll(t, HEAD_DIM // 2, 1) * sin_signed


def _ffn(h, w_gate, w_up, w_down, side_casts=()):
    d_ff = w_gate.shape[1]
    tm, tn, tn_tail = 2048, 256, 128
    n_main = d_ff // tn * tn
    a, *copies = _staged_matmul([h], [w_gate, w_up], n=n_main, out_cols=d_ff, out_dtype=BF16,
                                epilogue=_swiglu_epilogue, name="ffn_gate_up", tm=tm, tn=tn,
                                side_casts=side_casts)
    if n_main < d_ff:
        a, = _staged_matmul([h], [w_gate, w_up], n=d_ff - n_main, col0=n_main, into=a,
                            out_dtype=BF16, epilogue=_swiglu_epilogue, name="ffn_gate_up_tail",
                            tm=tm, tn=tn_tail)
    y, = _staged_matmul([a], [w_down], n=w_down.shape[1], out_dtype=F32,
                        epilogue=_store_epilogue, name="ffn_down", tm=512, tn=512)
    return y, copies


def _in_proj_pool(h, w, tm=1024, tn=512):
    return _staged_matmul([h], [w], n=POOL_WIDTH, out_dtype=F32, epilogue=_store_epilogue,
                          name="mix_in_proj_pool", tm=tm, tn=tn, col0=ATTN_WIDTH + 2 * KV_WIDTH)[0]


def _resident(shape):
    return pl.BlockSpec(shape, lambda t: (0,) * len(shape), pipeline_mode=pl.Buffered(1))


def _residual_qkv_kernel(x_ref, y_ref, gpost_ref, gnext_ref, w_ref, cos_ref, sin_ref,
                         xo_ref, h_ref, qkv_ref, *, weight):
    xn = x_ref[...] + weight * _rms(y_ref[...], gpost_ref[...])
    xo_ref[...] = xn
    h = _rms(xn, gnext_ref[...]).astype(BF16)
    h_ref[...] = h
    acc = jnp.dot(h, w_ref[...], preferred_element_type=F32)
    for c in range(acc.shape[1] // HEAD_DIM):
        cols = slice(c * HEAD_DIM, (c + 1) * HEAD_DIM)
        if c * HEAD_DIM < ATTN_WIDTH + KV_WIDTH:
            kind = 0 if c * HEAD_DIM < ATTN_WIDTH else 1
            val = _rope(acc[:, cols], cos_ref[kind], sin_ref[kind])
        else:
            val = acc[:, cols]
        qkv_ref[:, cols] = val.astype(qkv_ref.dtype)


def _residual_qkv(x, y, g_post, g_next, w_qkv, cos2, sin2, seq, weight, tm=128):
    m, d = x.shape
    n = ATTN_WIDTH + 2 * KV_WIDTH
    assert w_qkv.shape == (d, n) and w_qkv.dtype == BF16 and m % tm == 0 and seq % tm == 0
    row = lambda width: pl.BlockSpec((tm, width), lambda t: (t, 0))
    vec = pl.BlockSpec((1, d), lambda t: (0, 0))
    table = pl.BlockSpec((2, tm, HEAD_DIM), lambda t: (0, t % (seq // tm), 0))
    return pl.pallas_call(
        functools.partial(_residual_qkv_kernel, weight=weight),
        grid=(m // tm,),
        in_specs=[row(d), row(d), vec, vec, _resident((d, n)), table, table],
        out_specs=[row(d), row(d), row(n)],
        out_shape=[jax.ShapeDtypeStruct((m, d), F32), jax.ShapeDtypeStruct((m, d), BF16),
                   jax.ShapeDtypeStruct((m, n), BF16)],
        compiler_params=_params(1),
        name="residual_qkv",
    )(x, y, g_post, g_next, w_qkv, cos2, sin2)


def _out_proj_residual_kernel(a_ref, p_ref, x_ref, gpost_ref, gnext_ref, w_ref, xo_ref, h_ref):
    ka = a_ref.shape[1]
    y = (jnp.dot(a_ref[...], w_ref[0:ka, :], preferred_element_type=F32)
         + jnp.dot(p_ref[...], w_ref[ka:, :], preferred_element_type=F32))
    xn = x_ref[...] + _rms(y, gpost_ref[...])
    xo_ref[...] = xn
    h_ref[...] = _rms(xn, gnext_ref[...]).astype(h_ref.dtype)


def _out_proj_residual(a, pm, x, g_post, g_next, w_bf16, tm=128):
    m, d = x.shape
    k = a.shape[1] + pm.shape[1]
    assert w_bf16.shape == (k, d) and w_bf16.dtype == BF16 and m % tm == 0
    row = lambda width: pl.BlockSpec((tm, width), lambda t: (t, 0))
    vec = pl.BlockSpec((1, d), lambda t: (0, 0))
    return pl.pallas_call(
        _out_proj_residual_kernel,
        grid=(m // tm,),
        in_specs=[row(a.shape[1]), row(pm.shape[1]), row(d), vec, vec, _resident((k, d))],
        out_specs=[row(d), row(d)],
        out_shape=[jax.ShapeDtypeStruct((m, d), F32), jax.ShapeDtypeStruct((m, d), BF16)],
        compiler_params=_params(1),
        name="out_proj_residual",
    )(a, pm, x, g_post, g_next, w_bf16)


def _attn_kernel(sinks_ref, q_ref, k_ref, v_ref, o_ref, bias_s, *, seq):
    kvh = pl.program_id(1)
    g, blk = GQA_GROUP, BLOCK
    key = lax.broadcasted_iota(jnp.int32, (2 * blk, blk), 0)
    qry = lax.broadcasted_iota(jnp.int32, (2 * blk, blk), 1)
    cur_ok = (key >= blk) & (key - blk <= qry)
    prev_ok = (key < blk) & (key > qry)
    bias_s[0] = jnp.where(cur_ok, 0.0, -jnp.inf)
    bias_s[1] = jnp.where(cur_ok | prev_ok, 0.0, -jnp.inf)
    sinks = [jnp.full((1, blk), sinks_ref[kvh * g + c], F32) for c in range(g)]
    contract_last = (((1,), (1,)), ((), ()))
    contract_first = (((0,), (0,)), ((), ()))

    def body(n, carry):
        r0 = pl.multiple_of(n * blk, blk)
        rp = pl.multiple_of(jnp.maximum(n - 1, 0) * blk, blk)
        kw = jnp.concatenate([k_ref[pl.ds(rp, blk), :], k_ref[pl.ds(r0, blk), :]], axis=0)
        vw = jnp.concatenate([v_ref[pl.ds(rp, blk), :], v_ref[pl.ds(r0, blk), :]], axis=0)
        bias = bias_s[jnp.minimum(n, 1)]
        for c in range(g):
            qc = q_ref[pl.ds(r0, blk), c * HEAD_DIM:(c + 1) * HEAD_DIM]
            s = lax.dot_general(kw, qc, contract_last, preferred_element_type=F32) + bias
            m = jnp.maximum(jnp.max(s, axis=0, keepdims=True), sinks[c])
            e = jnp.exp(s - m)
            denom = jnp.sum(e, axis=0, keepdims=True) + jnp.exp(sinks[c] - m)
            p = (e * (1.0 / denom)).astype(BF16)
            o = lax.dot_general(p, vw, contract_first, preferred_element_type=F32)
            o_ref[pl.ds(r0, blk), c * HEAD_DIM:(c + 1) * HEAD_DIM] = o.astype(o_ref.dtype)
        return carry

    lax.fori_loop(0, seq // blk, body, 0, unroll=16)


def _attention(qkv, sinks, batch, seq):
    m = qkv.shape[0]
    qw = GQA_GROUP * HEAD_DIM
    k_blk0 = ATTN_WIDTH // HEAD_DIM
    v_blk0 = (ATTN_WIDTH + KV_WIDTH) // HEAD_DIM
    return pl.pallas_call(
        functools.partial(_attn_kernel, seq=seq),
        grid=(batch, N_KV_HEADS),
        in_specs=[pl.BlockSpec(memory_space=pltpu.SMEM),
                  pl.BlockSpec((seq, qw), lambda b, h: (b, h)),
                  pl.BlockSpec((seq, HEAD_DIM), lambda b, h: (b, k_blk0 + h)),
                  pl.BlockSpec((seq, HEAD_DIM), lambda b, h: (b, v_blk0 + h))],
        out_specs=pl.BlockSpec((seq, qw), lambda b, h: (b, h)),
        out_shape=jax.ShapeDtypeStruct((m, ATTN_WIDTH), BF16),
        scratch_shapes=[pltpu.VMEM((2, 2 * BLOCK, BLOCK), F32)],
        compiler_params=_params(2),
        name="swa_attention",
    )(sinks, qkv, qkv, qkv)


def _pool_kernel(p_ref, w_ref, sc_ref, o_ref, pad_s, y_s, *, seq, rows):
    grp = pl.program_id(1)
    gw = POOL_GROUP_WIDTH
    pad_s[0:POOL_HALO, :] = jnp.zeros((POOL_HALO, gw), F32)
    pad_s[POOL_HALO:, :] = p_ref[...]
    ext = rows + POOL_HALO
    lrow = lax.broadcasted_iota(jnp.int32, (rows, 128), 0)

    for gi, w in enumerate(POOL_WINDOWS):
        assert w & (w - 1) == 0 and w - 1 <= POOL_HALO

        @pl.when(grp == gi)
        def _(w=w):
            def body(r, carry):
                r0 = pl.multiple_of(r * rows, rows)
                cnt = jnp.minimum(lrow + (r0 + 1), w).astype(F32)
                for c in range(gw // 128):
                    x = pad_s[pl.ds(r0, ext), c * 128:(c + 1) * 128]
                    s = x
                    d = 1
                    while d < w:
                        s = s + pltpu.roll(s, d, 0)
                        d *= 2
                    y = s[POOL_HALO:] / cnt - x[POOL_HALO:]
                    y_s[pl.ds(r0, rows), c * 128:(c + 1) * 128] = y.astype(BF16)
                return carry

            lax.fori_loop(0, seq // rows, body, 0)

    out = jnp.dot(y_s[...], w_ref[0].astype(BF16), preferred_element_type=F32) * sc_ref[...]
    o_ref[...] = out.astype(o_ref.dtype)


def _pool(proj, pool_w, pool_scale, batch, seq, rows=128):
    m = proj.shape[0]
    gw = POOL_GROUP_WIDTH
    ng = len(POOL_WINDOWS)
    return pl.pallas_call(
        functools.partial(_pool_kernel, seq=seq, rows=rows),
        grid=(batch, ng),
        in_specs=[pl.BlockSpec((seq, gw), lambda b, g: (b, g)),
                  pl.BlockSpec((1, gw, gw), lambda b, g: (g, 0, 0)),
                  pl.BlockSpec((1, gw), lambda b, g: (0, g))],
        out_specs=pl.BlockSpec((seq, gw), lambda b, g: (b, g)),
        out_shape=jax.ShapeDtypeStruct((m, POOL_WIDTH), BF16),
        scratch_shapes=[pltpu.VMEM((seq + POOL_HALO, gw), F32), pltpu.VMEM((seq, gw), BF16)],
        compiler_params=_params(2),
        name="multiscale_pool",
    )(proj, pool_w, pool_scale)


def _rope_tables(seq):
    pos = np.arange(seq, dtype=np.float32)
    inv_freq = (np.float32(ROPE_THETA)
                ** (-np.arange(0, HEAD_DIM, 2, dtype=np.float32) / np.float32(HEAD_DIM)))
    ang = (pos[:, None] * inv_freq[None, :]).astype(np.float32)
    cos, sin = np.cos(ang).astype(np.float32), np.sin(ang).astype(np.float32)
    cos = np.concatenate([cos, cos], axis=-1)
    sin = np.concatenate([-sin, sin], axis=-1)
    scale = np.float32(HEAD_DIM ** -0.5)
    return jnp.asarray(np.stack([cos * scale, cos])), jnp.asarray(np.stack([sin * scale, sin]))


def kernel(x, ffn1_pre_g, ffn1_w_gate, ffn1_w_up, ffn1_w_down, ffn1_post_g, mix_pre_g, w_in, attn_sinks, pool_w, pool_scale, w_out, mix_post_g, ffn2_pre_g, ffn2_w_gate, ffn2_w_up, ffn2_w_down, ffn2_post_g):
    b, s, d = x.shape
    depth = ffn1_pre_g.shape[0]
    cos2, sin2 = _rope_tables(s)
    vec = lambda g: g.reshape(1, -1)
    xf = x.reshape(b * s, d)
    h = _prenorm(xf, vec(ffn1_pre_g[0]))
    for l in range(depth):
        side = [(w_in[l], SIDE_CAST_ROWS, ATTN_WIDTH + 2 * KV_WIDTH),
                (w_out[l], SIDE_CAST_ROWS, d)]
        y, (w_qkv, w_o) = _ffn(h, ffn1_w_gate[l], ffn1_w_up[l], ffn1_w_down[l], side)
        xf, h, qkv = _residual_qkv(xf, y, vec(ffn1_post_g[l]), vec(mix_pre_g[l]), w_qkv,
                                   cos2, sin2, s, FFN_RES_WEIGHT)
        a = _attention(qkv, attn_sinks[l], b, s)
        pm = _pool(_in_proj_pool(h, w_in[l]), pool_w[l], vec(pool_scale[l]), b, s)
        xf, h = _out_proj_residual(a, pm, xf, vec(mix_post_g[l]), vec(ffn2_pre_g[l]), w_o)

        y, _ = _ffn(h, ffn2_w_gate[l], ffn2_w_up[l], ffn2_w_down[l])
        g_next = vec(ffn1_pre_g[l + 1]) if l + 1 < depth else None
        xf, h = _residual(xf, y, vec(ffn2_post_g[l]), g_next, FFN_RES_WEIGHT)
    return xf.reshape(b, s, d)
```

```python
import functools

import jax
import jax.numpy as jnp
import numpy as np
from jax import lax
from jax.experimental import pallas as pl
from jax.experimental.pallas import tpu as pltpu

D_MODEL = 4096
HEAD_DIM = 128
N_HEADS = 16
N_KV_HEADS = 4
GQA_GROUP = N_HEADS // N_KV_HEADS
ATTN_WIDTH = N_HEADS * HEAD_DIM
KV_WIDTH = N_KV_HEADS * HEAD_DIM
BLOCK = 128
ROPE_THETA = 10000.0
POOL_WIDTH = D_MODEL - ATTN_WIDTH
POOL_WINDOWS = (2, 4, 8, 16)
POOL_GROUP_WIDTH = POOL_WIDTH // len(POOL_WINDOWS)
IN_PROJ_WIDTH = ATTN_WIDTH + 2 * KV_WIDTH + POOL_WIDTH
FFN_RES_WEIGHT = 0.5
RMS_EPS = 1e-6
POOL_HALO = 16

BF16 = jnp.bfloat16
F32 = jnp.float32
BF16_SUBLANES = 16
SIDE_CAST_ROWS = 32

VMEM_LIMIT_BYTES = 56 * 1024 * 1024


def _params(n_axes):
    return pltpu.CompilerParams(
        dimension_semantics=("arbitrary",) * n_axes,
        vmem_limit_bytes=VMEM_LIMIT_BYTES,
    )


def _rms(x, g):
    x = x.astype(F32)
    ms = jnp.mean(x * x, axis=-1, keepdims=True)
    return x * lax.rsqrt(ms + RMS_EPS) * g


def _norm_kernel(x_ref, g_ref, h_ref):
    h_ref[...] = _rms(x_ref[...], g_ref[...]).astype(h_ref.dtype)


def _prenorm(x, g, tm=512):
    m, d = x.shape
    return pl.pallas_call(
        _norm_kernel,
        grid=(m // tm,),
        in_specs=[pl.BlockSpec((tm, d), lambda i: (i, 0)),
                  pl.BlockSpec((1, d), lambda i: (0, 0))],
        out_specs=pl.BlockSpec((tm, d), lambda i: (i, 0)),
        out_shape=jax.ShapeDtypeStruct((m, d), BF16),
        compiler_params=_params(1),
        name="prenorm",
    )(x, g)


def _residual_kernel(x_ref, y_ref, gpost_ref, *rest, weight, with_next):
    if with_next:
        gnext_ref, xo_ref, h_ref = rest
    else:
        (xo_ref,) = rest
    r = _rms(y_ref[...], gpost_ref[...])
    if weight != 1.0:
        r = weight * r
    xn = x_ref[...] + r
    xo_ref[...] = xn
    if with_next:
        h_ref[...] = _rms(xn, gnext_ref[...]).astype(h_ref.dtype)


def _residual(x, y, g_post, g_next, weight, tm=256):
    m, d = x.shape
    with_next = g_next is not None
    row = pl.BlockSpec((tm, d), lambda i: (i, 0))
    vec = pl.BlockSpec((1, d), lambda i: (0, 0))
    in_specs = [row, row, vec] + ([vec] if with_next else [])
    args = (x, y, g_post) + ((g_next,) if with_next else ())
    out_specs = [row] + ([row] if with_next else [])
    out_shape = [jax.ShapeDtypeStruct((m, d), F32)] + (
        [jax.ShapeDtypeStruct((m, d), BF16)] if with_next else [])
    outs = pl.pallas_call(
        functools.partial(_residual_kernel, weight=weight, with_next=with_next),
        grid=(m // tm,),
        in_specs=in_specs,
        out_specs=out_specs,
        out_shape=out_shape,
        compiler_params=_params(1),
        name="residual_norm",
    )(*args)
    return (outs[0], outs[1]) if with_next else (outs[0], None)


def _staged_kernel(*refs, n_x, n_w, n_alias, side_steps, nj, slab, epilogue):
    n_side = len(side_steps)
    n_in = n_x + n_w
    x_refs = refs[:n_x]
    w_refs = refs[n_x:n_in]
    side_in = refs[n_in:n_in + n_side]
    n_in += n_side + n_alias
    o_ref = refs[n_in]
    side_out = refs[n_in + 1:n_in + 1 + n_side]
    w_s = refs[n_in + 1 + n_side:]
    j = pl.program_id(0)
    i = pl.program_id(1)

    for src, dst, steps in zip(side_in, side_out, side_steps):
        @pl.when(j * pl.num_programs(1) + i < steps)
        def _(src=src, dst=dst):
            dst[...] = src[...].astype(dst.dtype)

    @pl.when(j < nj)
    def _():
        rows = pl.ds(pl.multiple_of(i * slab, slab), slab)
        for w_ref, s in zip(w_refs, w_s):
            s[j % 2, rows, :] = w_ref[...].astype(BF16)

    @pl.when(j > 0)
    def _():
        slot = (j + 1) % 2
        accs = []
        for s in w_s:
            acc, k0 = None, 0
            for x_ref in x_refs:
                kx = x_ref.shape[1]
                part = jnp.dot(x_ref[...], s[slot, k0:k0 + kx, :], preferred_element_type=F32)
                acc = part if acc is None else acc + part
                k0 += kx
            accs.append(acc)
        epilogue(accs, o_ref)


def _staged_matmul(xs, ws, *, n, out_dtype, epilogue, name, tm, tn, col0=0, out_cols=None,
                   into=None, side_casts=()):
    m = xs[0].shape[0]
    k = sum(x.shape[1] for x in xs)
    assert all(w.shape[0] == k for w in ws) and m % tm == 0 and col0 % tn == 0 and n % tn == 0
    ni, nj = m // tm, n // tn
    assert k % (ni * BF16_SUBLANES) == 0
    slab = k // ni
    cb = col0 // tn
    ob = cb if (out_cols or into is not None) else 0
    out_struct = jax.ShapeDtypeStruct((m, out_cols or n), out_dtype) if into is None else (
        jax.ShapeDtypeStruct(into.shape, into.dtype))
    side_in, side_specs, side_shapes, side_steps = [], [], [], []
    for arr, rows, cols in side_casts:
        steps = arr.shape[0] // rows
        assert arr.shape[0] % rows == 0 and rows % BF16_SUBLANES == 0 and steps <= (nj + 1) * ni
        side_in.append(arr)
        side_specs.append(pl.BlockSpec(
            (rows, cols), lambda j, i, steps=steps: (jnp.minimum(j * ni + i, steps - 1), 0)))
        side_shapes.append(jax.ShapeDtypeStruct((arr.shape[0], cols), BF16))
        side_steps.append(steps)
    row_tile = lambda j, i: jnp.where(j == 0, 0, i)
    x_specs = [pl.BlockSpec((tm, x.shape[1]), lambda j, i: (row_tile(j, i), 0)) for x in xs]
    w_spec = pl.BlockSpec((slab, tn), lambda j, i: (jnp.where(j < nj, i, ni - 1),
                                                    cb + jnp.minimum(j, nj - 1)))
    o_spec = pl.BlockSpec((tm, tn), lambda j, i: (row_tile(j, i), ob + jnp.maximum(j - 1, 0)))
    alias_in = [] if into is None else [into]
    n_in = len(xs) + len(ws) + len(side_in)
    kern = functools.partial(_staged_kernel, n_x=len(xs), n_w=len(ws), n_alias=len(alias_in),
                             side_steps=tuple(side_steps), nj=nj, slab=slab, epilogue=epilogue)
    outs = pl.pallas_call(
        kern,
        grid=(nj + 1, ni),
        in_specs=(x_specs + [w_spec] * len(ws) + side_specs
                  + [pl.BlockSpec(memory_space=pl.ANY)] * len(alias_in)),
        out_specs=[o_spec] + side_specs,
        out_shape=[out_struct] + side_shapes,
        scratch_shapes=[pltpu.VMEM((2, k, tn), BF16) for _ in ws],
        input_output_aliases={n_in: 0} if alias_in else {},
        compiler_params=_params(2),
        name=name,
    )(*xs, *ws, *side_in, *alias_in)
    return outs


def _store_epilogue(accs, o_ref):
    o_ref[...] = accs[0].astype(o_ref.dtype)


def _swiglu_epilogue(accs, o_ref):
    g, u = accs
    o_ref[...] = (jax.nn.silu(g) * u).astype(o_ref.dtype)


def _rope(t, cos, sin_signed):
    return t * cos + pltpu.roll(t, HEAD_DIM // 2, 1) * sin_signed


def _ffn(h, w_gate, w_up, w_down, side_casts=()):
    d_ff = w_gate.shape[1]
    tm, tn, tn_tail = 1024, 512, 256
    n_main = d_ff // tn * tn
    a, *copies = _staged_matmul([h], [w_gate, w_up], n=n_main, out_cols=d_ff, out_dtype=BF16,
                                epilogue=_swiglu_epilogue, name="ffn_gate_up", tm=tm, tn=tn,
                                side_casts=side_casts)
    if n_main < d_ff:
        a, = _staged_matmul([h], [w_gate, w_up], n=d_ff - n_main, col0=n_main, into=a,
                            out_dtype=BF16, epilogue=_swiglu_epilogue, name="ffn_gate_up_tail",
                            tm=tm, tn=tn_tail)
    y, = _staged_matmul([a], [w_down], n=w_down.shape[1], out_dtype=BF16,
                        epilogue=_store_epilogue, name="ffn_down", tm=512, tn=512)
    return y, copies


def _in_proj_pool(h, w, tm=1024, tn=512):
    return _staged_matmul([h], [w], n=POOL_WIDTH, out_dtype=F32, epilogue=_store_epilogue,
                          name="mix_in_proj_pool", tm=tm, tn=tn, col0=ATTN_WIDTH + 2 * KV_WIDTH)[0]


def _resident(shape):
    return pl.BlockSpec(shape, lambda t: (0,) * len(shape), pipeline_mode=pl.Buffered(1))


def _residual_qkv_kernel(x_ref, y_ref, gpost_ref, gnext_ref, w_ref, cos_ref, sin_ref,
                         xo_ref, h_ref, qkv_ref, *, weight):
    xn = x_ref[...] + weight * _rms(y_ref[...], gpost_ref[...])
    xo_ref[...] = xn
    h = _rms(xn, gnext_ref[...]).astype(BF16)
    h_ref[...] = h
    acc = jnp.dot(h, w_ref[...], preferred_element_type=F32)
    for c in range(acc.shape[1] // HEAD_DIM):
        cols = slice(c * HEAD_DIM, (c + 1) * HEAD_DIM)
        if c * HEAD_DIM < ATTN_WIDTH + KV_WIDTH:
            kind = 0 if c * HEAD_DIM < ATTN_WIDTH else 1
            val = _rope(acc[:, cols], cos_ref[kind], sin_ref[kind])
        else:
            val = acc[:, cols]
        qkv_ref[:, cols] = val.astype(qkv_ref.dtype)


def _residual_qkv(x, y, g_post, g_next, w_qkv, cos2, sin2, seq, weight, tm=128):
    m, d = x.shape
    n = ATTN_WIDTH + 2 * KV_WIDTH
    assert w_qkv.shape == (d, n) and w_qkv.dtype == BF16 and m % tm == 0 and seq % tm == 0
    row = lambda width: pl.BlockSpec((tm, width), lambda t: (t, 0))
    vec = pl.BlockSpec((1, d), lambda t: (0, 0))
    table = pl.BlockSpec((2, tm, HEAD_DIM), lambda t: (0, t % (seq // tm), 0))
    return pl.pallas_call(
        functools.partial(_residual_qkv_kernel, weight=weight),
        grid=(m // tm,),
        in_specs=[row(d), row(d), vec, vec, _resident((d, n)), table, table],
        out_specs=[row(d), row(d), row(n)],
        out_shape=[jax.ShapeDtypeStruct((m, d), F32), jax.ShapeDtypeStruct((m, d), BF16),
                   jax.ShapeDtypeStruct((m, n), BF16)],
        compiler_params=_params(1),
        name="residual_qkv",
    )(x, y, g_post, g_next, w_qkv, cos2, sin2)


def _out_proj_residual_kernel(a_ref, p_ref, x_ref, gpost_ref, gnext_ref, w_ref, xo_ref, h_ref):
    ka = a_ref.shape[1]
    y = (jnp.dot(a_ref[...], w_ref[0:ka, :], preferred_element_type=F32)
         + jnp.dot(p_ref[...], w_ref[ka:, :], preferred_element_type=F32))
    xn = x_ref[...] + _rms(y, gpost_ref[...])
    xo_ref[...] = xn
    h_ref[...] = _rms(xn, gnext_ref[...]).astype(h_ref.dtype)


def _out_proj_residual(a, pm, x, g_post, g_next, w_bf16, tm=128):
    m, d = x.shape
    k = a.shape[1] + pm.shape[1]
    assert w_bf16.shape == (k, d) and w_bf16.dtype == BF16 and m % tm == 0
    row = lambda width: pl.BlockSpec((tm, width), lambda t: (t, 0))
    vec = pl.BlockSpec((1, d), lambda t: (0, 0))
    return pl.pallas_call(
        _out_proj_residual_kernel,
        grid=(m // tm,),
        in_specs=[row(a.shape[1]), row(pm.shape[1]), row(d), vec, vec, _resident((k, d))],
        out_specs=[row(d), row(d)],
        out_shape=[jax.ShapeDtypeStruct((m, d), F32), jax.ShapeDtypeStruct((m, d), BF16)],
        compiler_params=_params(1),
        name="out_proj_residual",
    )(a, pm, x, g_post, g_next, w_bf16)


def _attn_kernel(sinks_ref, q_ref, k_ref, v_ref, o_ref, bias_s, *, seq):
    kvh = pl.program_id(1)
    g, blk = GQA_GROUP, BLOCK
    key = lax.broadcasted_iota(jnp.int32, (2 * blk, blk), 0)
    qry = lax.broadcasted_iota(jnp.int32, (2 * blk, blk), 1)
    cur_ok = (key >= blk) & (key - blk <= qry)
    prev_ok = (key < blk) & (key > qry)
    bias_s[0] = jnp.where(cur_ok, 0.0, -jnp.inf)
    bias_s[1] = jnp.where(cur_ok | prev_ok, 0.0, -jnp.inf)
    sinks = [jnp.full((1, blk), sinks_ref[kvh * g + c], F32) for c in range(g)]
    contract_last = (((1,), (1,)), ((), ()))
    contract_first = (((0,), (0,)), ((), ()))

    def body(n, carry):
        r0 = pl.multiple_of(n * blk, blk)
        rp = pl.multiple_of(jnp.maximum(n - 1, 0) * blk, blk)
        kw = jnp.concatenate([k_ref[pl.ds(rp, blk), :], k_ref[pl.ds(r0, blk), :]], axis=0)
        vw = jnp.concatenate([v_ref[pl.ds(rp, blk), :], v_ref[pl.ds(r0, blk), :]], axis=0)
        bias = bias_s[jnp.minimum(n, 1)]
        for c in range(g):
            qc = q_ref[pl.ds(r0, blk), c * HEAD_DIM:(c + 1) * HEAD_DIM]
            s = lax.dot_general(kw, qc, contract_last, preferred_element_type=F32) + bias
            m = jnp.maximum(jnp.max(s, axis=0, keepdims=True), sinks[c])
            e = jnp.exp(s - m)
            denom = jnp.sum(e, axis=0, keepdims=True) + jnp.exp(sinks[c] - m)
            p = (e * (1.0 / denom)).astype(BF16)
            o = lax.dot_general(p, vw, contract_first, preferred_element_type=F32)
            o_ref[pl.ds(r0, blk), c * HEAD_DIM:(c + 1) * HEAD_DIM] = o.astype(o_ref.dtype)
        return carry

    lax.fori_loop(0, seq // blk, body, 0, unroll=16)


def _attention(qkv, sinks, batch, seq):
    m = qkv.shape[0]
    qw = GQA_GROUP * HEAD_DIM
    k_blk0 = ATTN_WIDTH // HEAD_DIM
    v_blk0 = (ATTN_WIDTH + KV_WIDTH) // HEAD_DIM
    return pl.pallas_call(
        functools.partial(_attn_kernel, seq=seq),
        grid=(batch, N_KV_HEADS),
        in_specs=[pl.BlockSpec(memory_space=pltpu.SMEM),
                  pl.BlockSpec((seq, qw), lambda b, h: (b, h)),
                  pl.BlockSpec((seq, HEAD_DIM), lambda b, h: (b, k_blk0 + h)),
                  pl.BlockSpec((seq, HEAD_DIM), lambda b, h: (b, v_blk0 + h))],
        out_specs=pl.BlockSpec((seq, qw), lambda b, h: (b, h)),
        out_shape=jax.ShapeDtypeStruct((m, ATTN_WIDTH), BF16),
        scratch_shapes=[pltpu.VMEM((2, 2 * BLOCK, BLOCK), F32)],
        compiler_params=_params(2),
        name="swa_attention",
    )(sinks, qkv, qkv, qkv)


def _pool_kernel(p_ref, w_ref, sc_ref, o_ref, pad_s, y_s, *, seq, rows):
    grp = pl.program_id(1)
    gw = POOL_GROUP_WIDTH
    pad_s[0:POOL_HALO, :] = jnp.zeros((POOL_HALO, gw), F32)
    pad_s[POOL_HALO:, :] = p_ref[...]
    ext = rows + POOL_HALO
    lrow = lax.broadcasted_iota(jnp.int32, (rows, 128), 0)

    for gi, w in enumerate(POOL_WINDOWS):
        assert w & (w - 1) == 0 and w - 1 <= POOL_HALO

        @pl.when(grp == gi)
        def _(w=w):
            def body(r, carry):
                r0 = pl.multiple_of(r * rows, rows)
                cnt = jnp.minimum(lrow + (r0 + 1), w).astype(F32)
                for c in range(gw // 128):
                    x = pad_s[pl.ds(r0, ext), c * 128:(c + 1) * 128]
                    s = x
                    d = 1
                    while d < w:
                        s = s + pltpu.roll(s, d, 0)
                        d *= 2
                    y = s[POOL_HALO:] / cnt - x[POOL_HALO:]
                    y_s[pl.ds(r0, rows), c * 128:(c + 1) * 128] = y.astype(BF16)
                return carry

            lax.fori_loop(0, seq // rows, body, 0)

    out = jnp.dot(y_s[...], w_ref[0].astype(BF16), preferred_element_type=F32) * sc_ref[...]
    o_ref[...] = out.astype(o_ref.dtype)


def _pool(proj, pool_w, pool_scale, batch, seq, rows=128):
    m = proj.shape[0]
    gw = POOL_GROUP_WIDTH
    ng = len(POOL_WINDOWS)
    return pl.pallas_call(
        functools.partial(_pool_kernel, seq=seq, rows=rows),
        grid=(batch, ng),
        in_specs=[pl.BlockSpec((seq, gw), lambda b, g: (b, g)),
                  pl.BlockSpec((1, gw, gw), lambda b, g: (g, 0, 0)),
                  pl.BlockSpec((1, gw), lambda b, g: (0, g))],
        out_specs=pl.BlockSpec((seq, gw), lambda b, g: (b, g)),
        out_shape=jax.ShapeDtypeStruct((m, POOL_WIDTH), BF16),
        scratch_shapes=[pltpu.VMEM((seq + POOL_HALO, gw), F32), pltpu.VMEM((seq, gw), BF16)],
        compiler_params=_params(2),
        name="multiscale_pool",
    )(proj, pool_w, pool_scale)


def _rope_tables(seq):
    pos = np.arange(seq, dtype=np.float32)
    inv_freq = (np.float32(ROPE_THETA)
                ** (-np.arange(0, HEAD_DIM, 2, dtype=np.float32) / np.float32(HEAD_DIM)))
    ang = (pos[:, None] * inv_freq[None, :]).astype(np.float32)
    cos, sin = np.cos(ang).astype(np.float32), np.sin(ang).astype(np.float32)
    cos = np.concatenate([cos, cos], axis=-1)
    sin = np.concatenate([-sin, sin], axis=-1)
    scale = np.float32(HEAD_DIM ** -0.5)
    return jnp.asarray(np.stack([cos * scale, cos])), jnp.asarray(np.stack([sin * scale, sin]))


def kernel(x, ffn1_pre_g, ffn1_w_gate, ffn1_w_up, ffn1_w_down, ffn1_post_g, mix_pre_g, w_in, attn_sinks, pool_w, pool_scale, w_out, mix_post_g, ffn2_pre_g, ffn2_w_gate, ffn2_w_up, ffn2_w_down, ffn2_post_g):
    b, s, d = x.shape
    depth = ffn1_pre_g.shape[0]
    cos2, sin2 = _rope_tables(s)
    vec = lambda g: g.reshape(1, -1)
    xf = x.reshape(b * s, d)
    h = _prenorm(xf, vec(ffn1_pre_g[0]))
    for l in range(depth):
        side = [(w_in[l], SIDE_CAST_ROWS, ATTN_WIDTH + 2 * KV_WIDTH),
                (w_out[l], SIDE_CAST_ROWS, d)]
        y, (w_qkv, w_o) = _ffn(h, ffn1_w_gate[l], ffn1_w_up[l], ffn1_w_down[l], side)
        xf, h, qkv = _residual_qkv(xf, y, vec(ffn1_post_g[l]), vec(mix_pre_g[l]), w_qkv,
                                   cos2, sin2, s, FFN_RES_WEIGHT)
        a = _attention(qkv, attn_sinks[l], b, s)
        pm = _pool(_in_proj_pool(h, w_in[l]), pool_w[l], vec(pool_scale[l]), b, s)
        xf, h = _out_proj_residual(a, pm, xf, vec(mix_post_g[l]), vec(ffn2_pre_g[l]), w_o)

        y, _ = _ffn(h, ffn2_w_gate[l], ffn2_w_up[l], ffn2_w_down[l])
        g_next = vec(ffn1_pre_g[l + 1]) if l + 1 < depth else None
        xf, h = _residual(xf, y, vec(ffn2_post_g[l]), g_next, FFN_RES_WEIGHT)
    return xf.reshape(b, s, d)
```

```python
import functools

import jax
import jax.numpy as jnp
import numpy as np
from jax import lax
from jax.experimental import pallas as pl
from jax.experimental.pallas import tpu as pltpu

D_MODEL = 4096
HEAD_DIM = 128
N_HEADS = 16
N_KV_HEADS = 4
GQA_GROUP = N_HEADS // N_KV_HEADS
ATTN_WIDTH = N_HEADS * HEAD_DIM
KV_WIDTH = N_KV_HEADS * HEAD_DIM
BLOCK = 128
ROPE_THETA = 10000.0
POOL_WIDTH = D_MODEL - ATTN_WIDTH
POOL_WINDOWS = (2, 4, 8, 16)
POOL_GROUP_WIDTH = POOL_WIDTH // len(POOL_WINDOWS)
IN_PROJ_WIDTH = ATTN_WIDTH + 2 * KV_WIDTH + POOL_WIDTH
FFN_RES_WEIGHT = 0.5
RMS_EPS = 1e-6
POOL_HALO = 16

BF16 = jnp.bfloat16
F32 = jnp.float32
BF16_SUBLANES = 16
SIDE_CAST_ROWS = 32

VMEM_LIMIT_BYTES = 56 * 1024 * 1024


def _params(n_axes):
    return pltpu.CompilerParams(
        dimension_semantics=("arbitrary",) * n_axes,
        vmem_limit_bytes=VMEM_LIMIT_BYTES,
    )


def _rms(x, g):
    ms = jnp.mean(x * x, axis=-1, keepdims=True)
    return x * lax.rsqrt(ms + RMS_EPS) * g


def _norm_kernel(x_ref, g_ref, h_ref):
    h_ref[...] = _rms(x_ref[...], g_ref[...]).astype(h_ref.dtype)


def _prenorm(x, g, tm=512):
    m, d = x.shape
    return pl.pallas_call(
        _norm_kernel,
        grid=(m // tm,),
        in_specs=[pl.BlockSpec((tm, d), lambda i: (i, 0)),
                  pl.BlockSpec((1, d), lambda i: (0, 0))],
        out_specs=pl.BlockSpec((tm, d), lambda i: (i, 0)),
        out_shape=jax.ShapeDtypeStruct((m, d), BF16),
        compiler_params=_params(1),
        name="prenorm",
    )(x, g)


def _residual_kernel(x_ref, y_ref, gpost_ref, *rest, weight, with_next):
    if with_next:
        gnext_ref, xo_ref, h_ref = rest
    else:
        (xo_ref,) = rest
    r = _rms(y_ref[...], gpost_ref[...])
    if weight != 1.0:
        r = weight * r
    xn = x_ref[...] + r
    xo_ref[...] = xn
    if with_next:
        h_ref[...] = _rms(xn, gnext_ref[...]).astype(h_ref.dtype)


def _residual(x, y, g_post, g_next, weight, tm=256):
    m, d = x.shape
    with_next = g_next is not None
    row = pl.BlockSpec((tm, d), lambda i: (i, 0))
    vec = pl.BlockSpec((1, d), lambda i: (0, 0))
    in_specs = [row, row, vec] + ([vec] if with_next else [])
    args = (x, y, g_post) + ((g_next,) if with_next else ())
    out_specs = [row] + ([row] if with_next else [])
    out_shape = [jax.ShapeDtypeStruct((m, d), F32)] + (
        [jax.ShapeDtypeStruct((m, d), BF16)] if with_next else [])
    outs = pl.pallas_call(
        functools.partial(_residual_kernel, weight=weight, with_next=with_next),
        grid=(m // tm,),
        in_specs=in_specs,
        out_specs=out_specs,
        out_shape=out_shape,
        compiler_params=_params(1),
        name="residual_norm",
    )(*args)
    return (outs[0], outs[1]) if with_next else (outs[0], None)


def _staged_kernel(*refs, n_x, n_w, n_alias, side_steps, nj, slab, epilogue):
    n_side = len(side_steps)
    n_in = n_x + n_w
    x_refs = refs[:n_x]
    w_refs = refs[n_x:n_in]
    side_in = refs[n_in:n_in + n_side]
    n_in += n_side + n_alias
    o_ref = refs[n_in]
    side_out = refs[n_in + 1:n_in + 1 + n_side]
    w_s = refs[n_in + 1 + n_side:]
    j = pl.program_id(0)
    i = pl.program_id(1)

    for src, dst, steps in zip(side_in, side_out, side_steps):
        @pl.when(j * pl.num_programs(1) + i < steps)
        def _(src=src, dst=dst):
            dst[...] = src[...].astype(dst.dtype)

    @pl.when(j < nj)
    def _():
        rows = pl.ds(pl.multiple_of(i * slab, slab), slab)
        for w_ref, s in zip(w_refs, w_s):
            s[j % 2, rows, :] = w_ref[...].astype(BF16)

    @pl.when(j > 0)
    def _():
        slot = (j + 1) % 2
        accs = []
        for s in w_s:
            acc, k0 = None, 0
            for x_ref in x_refs:
                kx = x_ref.shape[1]
                part = jnp.dot(x_ref[...], s[slot, k0:k0 + kx, :], preferred_element_type=F32)
                acc = part if acc is None else acc + part
                k0 += kx
            accs.append(acc)
        epilogue(accs, o_ref)


def _staged_matmul(xs, ws, *, n, out_dtype, epilogue, name, tm, tn, col0=0, out_cols=None,
                   into=None, side_casts=()):
    m = xs[0].shape[0]
    k = sum(x.shape[1] for x in xs)
    assert all(w.shape[0] == k for w in ws) and m % tm == 0 and col0 % tn == 0 and n % tn == 0
    ni, nj = m // tm, n // tn
    assert k % (ni * BF16_SUBLANES) == 0
    slab = k // ni
    cb = col0 // tn
    ob = cb if (out_cols or into is not None) else 0
    out_struct = jax.ShapeDtypeStruct((m, out_cols or n), out_dtype) if into is None else (
        jax.ShapeDtypeStruct(into.shape, into.dtype))
    side_in, side_specs, side_shapes, side_steps = [], [], [], []
    for arr, rows, cols in side_casts:
        steps = arr.shape[0] // rows
        assert arr.shape[0] % rows == 0 and rows % BF16_SUBLANES == 0 and steps <= (nj + 1) * ni
        side_in.append(arr)
        side_specs.append(pl.BlockSpec(
            (rows, cols), lambda j, i, steps=steps: (jnp.minimum(j * ni + i, steps - 1), 0)))
        side_shapes.append(jax.ShapeDtypeStruct((arr.shape[0], cols), BF16))
        side_steps.append(steps)
    row_tile = lambda j, i: jnp.where(j == 0, 0, i)
    x_specs = [pl.BlockSpec((tm, x.shape[1]), lambda j, i: (row_tile(j, i), 0)) for x in xs]
    w_spec = pl.BlockSpec((slab, tn), lambda j, i: (jnp.where(j < nj, i, ni - 1),
                                                    cb + jnp.minimum(j, nj - 1)))
    o_spec = pl.BlockSpec((tm, tn), lambda j, i: (row_tile(j, i), ob + jnp.maximum(j - 1, 0)))
    alias_in = [] if into is None else [into]
    n_in = len(xs) + len(ws) + len(side_in)
    kern = functools.partial(_staged_kernel, n_x=len(xs), n_w=len(ws), n_alias=len(alias_in),
                             side_steps=tuple(side_steps), nj=nj, slab=slab, epilogue=epilogue)
    outs = pl.pallas_call(
        kern,
        grid=(nj + 1, ni),
        in_specs=(x_specs + [w_spec] * len(ws) + side_specs
                  + [pl.BlockSpec(memory_space=pl.ANY)] * len(alias_in)),
        out_specs=[o_spec] + side_specs,
        out_shape=[out_struct] + side_shapes,
        scratch_shapes=[pltpu.VMEM((2, k, tn), BF16) for _ in ws],
        input_output_aliases={n_in: 0} if alias_in else {},
        compiler_params=_params(2),
        name=name,
    )(*xs, *ws, *side_in, *alias_in)
    return outs


def _store_epilogue(accs, o_ref):
    o_ref[...] = accs[0].astype(o_ref.dtype)


def _swiglu_epilogue(accs, o_ref):
    g, u = accs
    o_ref[...] = (jax.nn.silu(g) * u).astype(o_ref.dtype)


def _rope(t, cos, sin_signed):
    return t * cos + pltpu.roll(t, HEAD_DIM // 2, 1) * sin_signed


def _ffn(h, w_gate, w_up, w_down, side_casts=()):
    d_ff = w_gate.shape[1]
    tm, tn, tn_tail = 1024, 512, 256
    n_main = d_ff // tn * tn
    a, *copies = _staged_matmul([h], [w_gate, w_up], n=n_main, out_cols=d_ff, out_dtype=BF16,
                                epilogue=_swiglu_epilogue, name="ffn_gate_up", tm=tm, tn=tn,
                                side_casts=side_casts)
    if n_main < d_ff:
        a, = _staged_matmul([h], [w_gate, w_up], n=d_ff - n_main, col0=n_main, into=a,
                            out_dtype=BF16, epilogue=_swiglu_epilogue, name="ffn_gate_up_tail",
                            tm=2 * tm, tn=tn_tail)
    y, = _staged_matmul([a], [w_down], n=w_down.shape[1], out_dtype=F32,
                        epilogue=_store_epilogue, name="ffn_down", tm=512, tn=512)
    return y, copies


def _in_proj_pool(h, w, tm=1024, tn=512):
    return _staged_matmul([h], [w], n=POOL_WIDTH, out_dtype=F32, epilogue=_store_epilogue,
                          name="mix_in_proj_pool", tm=tm, tn=tn, col0=ATTN_WIDTH + 2 * KV_WIDTH)[0]


def _resident(shape):
    return pl.BlockSpec(shape, lambda t: (0,) * len(shape), pipeline_mode=pl.Buffered(1))


def _residual_qkv_kernel(x_ref, y_ref, gpost_ref, gnext_ref, w_ref, cos_ref, sin_ref,
                         xo_ref, h_ref, qkv_ref, *, weight):
    xn = x_ref[...] + weight * _rms(y_ref[...], gpost_ref[...])
    xo_ref[...] = xn
    h = _rms(xn, gnext_ref[...]).astype(BF16)
    h_ref[...] = h
    acc = jnp.dot(h, w_ref[...], preferred_element_type=F32)
    for c in range(acc.shape[1] // HEAD_DIM):
        cols = slice(c * HEAD_DIM, (c + 1) * HEAD_DIM)
        if c * HEAD_DIM < ATTN_WIDTH + KV_WIDTH:
            kind = 0 if c * HEAD_DIM < ATTN_WIDTH else 1
            val = _rope(acc[:, cols], cos_ref[kind], sin_ref[kind])
        else:
            val = acc[:, cols]
        qkv_ref[:, cols] = val.astype(qkv_ref.dtype)


def _residual_qkv(x, y, g_post, g_next, w_qkv, cos2, sin2, seq, weight, tm=128):
    m, d = x.shape
    n = ATTN_WIDTH + 2 * KV_WIDTH
    assert w_qkv.shape == (d, n) and w_qkv.dtype == BF16 and m % tm == 0 and seq % tm == 0
    row = lambda width: pl.BlockSpec((tm, width), lambda t: (t, 0))
    vec = pl.BlockSpec((1, d), lambda t: (0, 0))
    table = pl.BlockSpec((2, tm, HEAD_DIM), lambda t: (0, t % (seq // tm), 0))
    return pl.pallas_call(
        functools.partial(_residual_qkv_kernel, weight=weight),
        grid=(m // tm,),
        in_specs=[row(d), row(d), vec, vec, _resident((d, n)), table, table],
        out_specs=[row(d), row(d), row(n)],
        out_shape=[jax.ShapeDtypeStruct((m, d), F32), jax.ShapeDtypeStruct((m, d), BF16),
                   jax.ShapeDtypeStruct((m, n), BF16)],
        compiler_params=_params(1),
        name="residual_qkv",
    )(x, y, g_post, g_next, w_qkv, cos2, sin2)


def _out_proj_residual_kernel(a_ref, p_ref, x_ref, gpost_ref, gnext_ref, w_ref, xo_ref, h_ref):
    ka = a_ref.shape[1]
    y = (jnp.dot(a_ref[...], w_ref[0:ka, :], preferred_element_type=F32)
         + jnp.dot(p_ref[...], w_ref[ka:, :], preferred_element_type=F32))
    xn = x_ref[...] + _rms(y, gpost_ref[...])
    xo_ref[...] = xn
    h_ref[...] = _rms(xn, gnext_ref[...]).astype(h_ref.dtype)


def _out_proj_residual(a, pm, x, g_post, g_next, w_bf16, tm=128):
    m, d = x.shape
    k = a.shape[1] + pm.shape[1]
    assert w_bf16.shape == (k, d) and w_bf16.dtype == BF16 and m % tm == 0
    row = lambda width: pl.BlockSpec((tm, width), lambda t: (t, 0))
    vec = pl.BlockSpec((1, d), lambda t: (0, 0))
    return pl.pallas_call(
        _out_proj_residual_kernel,
        grid=(m // tm,),
        in_specs=[row(a.shape[1]), row(pm.shape[1]), row(d), vec, vec, _resident((k, d))],
        out_specs=[row(d), row(d)],
        out_shape=[jax.ShapeDtypeStruct((m, d), F32), jax.ShapeDtypeStruct((m, d), BF16)],
        compiler_params=_params(1),
        name="out_proj_residual",
    )(a, pm, x, g_post, g_next, w_bf16)


def _attn_kernel(sinks_ref, q_ref, k_ref, v_ref, o_ref, bias_s, *, seq):
    kvh = pl.program_id(1)
    g, blk = GQA_GROUP, BLOCK
    key = lax.broadcasted_iota(jnp.int32, (2 * blk, blk), 0)
    qry = lax.broadcasted_iota(jnp.int32, (2 * blk, blk), 1)
    cur_ok = (key >= blk) & (key - blk <= qry)
    prev_ok = (key < blk) & (key > qry)
    bias_s[0] = jnp.where(cur_ok, 0.0, -jnp.inf)
    bias_s[1] = jnp.where(cur_ok | prev_ok, 0.0, -jnp.inf)
    sinks = [jnp.full((1, blk), sinks_ref[kvh * g + c], F32) for c in range(g)]
    contract_last = (((1,), (1,)), ((), ()))
    contract_first = (((0,), (0,)), ((), ()))

    def body(n, carry):
        r0 = pl.multiple_of(n * blk, blk)
        rp = pl.multiple_of(jnp.maximum(n - 1, 0) * blk, blk)
        kw = jnp.concatenate([k_ref[pl.ds(rp, blk), :], k_ref[pl.ds(r0, blk), :]], axis=0)
        vw = jnp.concatenate([v_ref[pl.ds(rp, blk), :], v_ref[pl.ds(r0, blk), :]], axis=0)
        bias = bias_s[jnp.minimum(n, 1)]
        for c in range(g):
            qc = q_ref[pl.ds(r0, blk), c * HEAD_DIM:(c + 1) * HEAD_DIM]
            s = lax.dot_general(kw, qc, contract_last, preferred_element_type=F32) + bias
            m = jnp.maximum(jnp.max(s, axis=0, keepdims=True), sinks[c])
            e = jnp.exp(s - m)
            denom = jnp.sum(e, axis=0, keepdims=True) + jnp.exp(sinks[c] - m)
            p = (e * (1.0 / denom)).astype(BF16)
            o = lax.dot_general(p, vw, contract_first, preferred_element_type=F32)
            o_ref[pl.ds(r0, blk), c * HEAD_DIM:(c + 1) * HEAD_DIM] = o.astype(o_ref.dtype)
        return carry

    lax.fori_loop(0, seq // blk, body, 0, unroll=16)


def _attention(qkv, sinks, batch, seq):
    m = qkv.shape[0]
    qw = GQA_GROUP * HEAD_DIM
    k_blk0 = ATTN_WIDTH // HEAD_DIM
    v_blk0 = (ATTN_WIDTH + KV_WIDTH) // HEAD_DIM
    return pl.pallas_call(
        functools.partial(_attn_kernel, seq=seq),
        grid=(batch, N_KV_HEADS),
        in_specs=[pl.BlockSpec(memory_space=pltpu.SMEM),
                  pl.BlockSpec((seq, qw), lambda b, h: (b, h)),
                  pl.BlockSpec((seq, HEAD_DIM), lambda b, h: (b, k_blk0 + h)),
                  pl.BlockSpec((seq, HEAD_DIM), lambda b, h: (b, v_blk0 + h))],
        out_specs=pl.BlockSpec((seq, qw), lambda b, h: (b, h)),
        out_shape=jax.ShapeDtypeStruct((m, ATTN_WIDTH), BF16),
        scratch_shapes=[pltpu.VMEM((2, 2 * BLOCK, BLOCK), F32)],
        compiler_params=_params(2),
        name="swa_attention",
    )(sinks, qkv, qkv, qkv)


def _pool_kernel(p_ref, w_ref, sc_ref, o_ref, pad_s, y_s, *, seq, rows):
    grp = pl.program_id(1)
    gw = POOL_GROUP_WIDTH
    pad_s[0:POOL_HALO, :] = jnp.zeros((POOL_HALO, gw), F32)
    pad_s[POOL_HALO:, :] = p_ref[...]
    ext = rows + POOL_HALO
    lrow = lax.broadcasted_iota(jnp.int32, (rows, 128), 0)

    for gi, w in enumerate(POOL_WINDOWS):
        assert w & (w - 1) == 0 and w - 1 <= POOL_HALO

        @pl.when(grp == gi)
        def _(w=w):
            def body(r, carry):
                r0 = pl.multiple_of(r * rows, rows)
                cnt = jnp.minimum(lrow + (r0 + 1), w).astype(F32)
                for c in range(gw // 128):
                    x = pad_s[pl.ds(r0, ext), c * 128:(c + 1) * 128]
                    s = x
                    d = 1
                    while d < w:
                        s = s + pltpu.roll(s, d, 0)
                        d *= 2
                    y = s[POOL_HALO:] / cnt - x[POOL_HALO:]
                    y_s[pl.ds(r0, rows), c * 128:(c + 1) * 128] = y.astype(BF16)
                return carry

            lax.fori_loop(0, seq // rows, body, 0)

    out = jnp.dot(y_s[...], w_ref[0].astype(BF16), preferred_element_type=F32) * sc_ref[...]
    o_ref[...] = out.astype(o_ref.dtype)


def _pool(proj, pool_w, pool_scale, batch, seq, rows=128):
    m = proj.shape[0]
    gw = POOL_GROUP_WIDTH
    ng = len(POOL_WINDOWS)
    return pl.pallas_call(
        functools.partial(_pool_kernel, seq=seq, rows=rows),
        grid=(batch, ng),
        in_specs=[pl.BlockSpec((seq, gw), lambda b, g: (b, g)),
                  pl.BlockSpec((1, gw, gw), lambda b, g: (g, 0, 0)),
                  pl.BlockSpec((1, gw), lambda b, g: (0, g))],
        out_specs=pl.BlockSpec((seq, gw), lambda b, g: (b, g)),
        out_shape=jax.ShapeDtypeStruct((m, POOL_WIDTH), BF16),
        scratch_shapes=[pltpu.VMEM((seq + POOL_HALO, gw), F32), pltpu.VMEM((seq, gw), BF16)],
        compiler_params=_params(2),
        name="multiscale_pool",
    )(proj, pool_w, pool_scale)


def _rope_tables(seq):
    pos = np.arange(seq, dtype=np.float32)
    inv_freq = (np.float32(ROPE_THETA)
                ** (-np.arange(0, HEAD_DIM, 2, dtype=np.float32) / np.float32(HEAD_DIM)))
    ang = (pos[:, None] * inv_freq[None, :]).astype(np.float32)
    cos, sin = np.cos(ang).astype(np.float32), np.sin(ang).astype(np.float32)
    cos = np.concatenate([cos, cos], axis=-1)
    sin = np.concatenate([-sin, sin], axis=-1)
    scale = np.float32(HEAD_DIM ** -0.5)
    return jnp.asarray(np.stack([cos * scale, cos])), jnp.asarray(np.stack([sin * scale, sin]))


def kernel(x, ffn1_pre_g, ffn1_w_gate, ffn1_w_up, ffn1_w_down, ffn1_post_g, mix_pre_g, w_in, attn_sinks, pool_w, pool_scale, w_out, mix_post_g, ffn2_pre_g, ffn2_w_gate, ffn2_w_up, ffn2_w_down, ffn2_post_g):
    b, s, d = x.shape
    depth = ffn1_pre_g.shape[0]
    cos2, sin2 = _rope_tables(s)
    vec = lambda g: g.reshape(1, -1)
    xf = x.reshape(b * s, d)
    h = _prenorm(xf, vec(ffn1_pre_g[0]))
    for l in range(depth):
        side = [(w_in[l], SIDE_CAST_ROWS, ATTN_WIDTH + 2 * KV_WIDTH),
                (w_out[l], SIDE_CAST_ROWS, d)]
        y, (w_qkv, w_o) = _ffn(h, ffn1_w_gate[l], ffn1_w_up[l], ffn1_w_down[l], side)
        xf, h, qkv = _residual_qkv(xf, y, vec(ffn1_post_g[l]), vec(mix_pre_g[l]), w_qkv,
                                   cos2, sin2, s, FFN_RES_WEIGHT)
        a = _attention(qkv, attn_sinks[l], b, s)
        pm = _pool(_in_proj_pool(h, w_in[l]), pool_w[l], vec(pool_scale[l]), b, s)
        xf, h = _out_proj_residual(a, pm, xf, vec(mix_post_g[l]), vec(ffn2_pre_g[l]), w_o)

        y, _ = _ffn(h, ffn2_w_gate[l], ffn2_w_up[l], ffn2_w_down[l])
        g_next = vec(ffn1_pre_g[l + 1]) if l + 1 < depth else None
        xf, h = _residual(xf, y, vec(ffn2_post_g[l]), g_next, FFN_RES_WEIGHT)
    return xf.reshape(b, s, d)
```

```python
import functools

import jax
import jax.numpy as jnp
import numpy as np
from jax import lax
from jax.experimental import pallas as pl
from jax.experimental.pallas import tpu as pltpu

D_MODEL = 4096
HEAD_DIM = 128
N_HEADS = 16
N_KV_HEADS = 4
GQA_GROUP = N_HEADS // N_KV_HEADS
ATTN_WIDTH = N_HEADS * HEAD_DIM
KV_WIDTH = N_KV_HEADS * HEAD_DIM
BLOCK = 128
ROPE_THETA = 10000.0
POOL_WIDTH = D_MODEL - ATTN_WIDTH
POOL_WINDOWS = (2, 4, 8, 16)
POOL_GROUP_WIDTH = POOL_WIDTH // len(POOL_WINDOWS)
IN_PROJ_WIDTH = ATTN_WIDTH + 2 * KV_WIDTH + POOL_WIDTH
FFN_RES_WEIGHT = 0.5
RMS_EPS = 1e-6
POOL_HALO = 16

BF16 = jnp.bfloat16
F32 = jnp.float32
BF16_SUBLANES = 16
SIDE_CAST_ROWS = 32

VMEM_LIMIT_BYTES = 62 * 1024 * 1024


def _params(n_axes):
    return pltpu.CompilerParams(
        dimension_semantics=("arbitrary",) * n_axes,
        vmem_limit_bytes=VMEM_LIMIT_BYTES,
    )


def _rms(x, g):
    ms = jnp.mean(x * x, axis=-1, keepdims=True)
    return x * lax.rsqrt(ms + RMS_EPS) * g


def _norm_kernel(x_ref, g_ref, h_ref):
    h_ref[...] = _rms(x_ref[...], g_ref[...]).astype(h_ref.dtype)


def _prenorm(x, g, tm=512):
    m, d = x.shape
    return pl.pallas_call(
        _norm_kernel,
        grid=(m // tm,),
        in_specs=[pl.BlockSpec((tm, d), lambda i: (i, 0)),
                  pl.BlockSpec((1, d), lambda i: (0, 0))],
        out_specs=pl.BlockSpec((tm, d), lambda i: (i, 0)),
        out_shape=jax.ShapeDtypeStruct((m, d), BF16),
        compiler_params=_params(1),
        name="prenorm",
    )(x, g)


def _residual_kernel(x_ref, y_ref, gpost_ref, *rest, weight, with_next):
    if with_next:
        gnext_ref, xo_ref, h_ref = rest
    else:
        (xo_ref,) = rest
    r = _rms(y_ref[...], gpost_ref[...])
    if weight != 1.0:
        r = weight * r
    xn = x_ref[...] + r
    xo_ref[...] = xn
    if with_next:
        h_ref[...] = _rms(xn, gnext_ref[...]).astype(h_ref.dtype)


def _residual(x, y, g_post, g_next, weight, tm=256):
    m, d = x.shape
    with_next = g_next is not None
    row = pl.BlockSpec((tm, d), lambda i: (i, 0))
    vec = pl.BlockSpec((1, d), lambda i: (0, 0))
    in_specs = [row, row, vec] + ([vec] if with_next else [])
    args = (x, y, g_post) + ((g_next,) if with_next else ())
    out_specs = [row] + ([row] if with_next else [])
    out_shape = [jax.ShapeDtypeStruct((m, d), F32)] + (
        [jax.ShapeDtypeStruct((m, d), BF16)] if with_next else [])
    outs = pl.pallas_call(
        functools.partial(_residual_kernel, weight=weight, with_next=with_next),
        grid=(m // tm,),
        in_specs=in_specs,
        out_specs=out_specs,
        out_shape=out_shape,
        compiler_params=_params(1),
        name="residual_norm",
    )(*args)
    return (outs[0], outs[1]) if with_next else (outs[0], None)


def _staged_kernel(*refs, n_x, n_w, n_alias, side_steps, nj, slab, epilogue):
    n_side = len(side_steps)
    n_in = n_x + n_w
    x_refs = refs[:n_x]
    w_refs = refs[n_x:n_in]
    side_in = refs[n_in:n_in + n_side]
    n_in += n_side + n_alias
    o_ref = refs[n_in]
    side_out = refs[n_in + 1:n_in + 1 + n_side]
    w_s = refs[n_in + 1 + n_side:]
    j = pl.program_id(0)
    i = pl.program_id(1)

    for src, dst, steps in zip(side_in, side_out, side_steps):
        @pl.when(j * pl.num_programs(1) + i < steps)
        def _(src=src, dst=dst):
            dst[...] = src[...].astype(dst.dtype)

    @pl.when(j < nj)
    def _():
        rows = pl.ds(pl.multiple_of(i * slab, slab), slab)
        for w_ref, s in zip(w_refs, w_s):
            s[j % 2, rows, :] = w_ref[...].astype(BF16)

    @pl.when(j > 0)
    def _():
        slot = (j + 1) % 2
        accs = []
        for s in w_s:
            acc, k0 = None, 0
            for x_ref in x_refs:
                kx = x_ref.shape[1]
                part = jnp.dot(x_ref[...], s[slot, k0:k0 + kx, :], preferred_element_type=F32)
                acc = part if acc is None else acc + part
                k0 += kx
            accs.append(acc)
        epilogue(accs, o_ref)


def _staged_matmul(xs, ws, *, n, out_dtype, epilogue, name, tm, tn, col0=0, out_cols=None,
                   into=None, side_casts=()):
    m = xs[0].shape[0]
    k = sum(x.shape[1] for x in xs)
    assert all(w.shape[0] == k for w in ws) and m % tm == 0 and col0 % tn == 0 and n % tn == 0
    ni, nj = m // tm, n // tn
    assert k % (ni * BF16_SUBLANES) == 0
    slab = k // ni
    cb = col0 // tn
    ob = cb if (out_cols or into is not None) else 0
    out_struct = jax.ShapeDtypeStruct((m, out_cols or n), out_dtype) if into is None else (
        jax.ShapeDtypeStruct(into.shape, into.dtype))
    side_in, side_specs, side_shapes, side_steps = [], [], [], []
    for arr, rows, cols in side_casts:
        steps = arr.shape[0] // rows
        assert arr.shape[0] % rows == 0 and rows % BF16_SUBLANES == 0 and steps <= (nj + 1) * ni
        side_in.append(arr)
        side_specs.append(pl.BlockSpec(
            (rows, cols), lambda j, i, steps=steps: (jnp.minimum(j * ni + i, steps - 1), 0)))
        side_shapes.append(jax.ShapeDtypeStruct((arr.shape[0], cols), BF16))
        side_steps.append(steps)
    row_tile = lambda j, i: jnp.where(j == 0, 0, i)
    x_specs = [pl.BlockSpec((tm, x.shape[1]), lambda j, i: (row_tile(j, i), 0)) for x in xs]
    w_spec = pl.BlockSpec((slab, tn), lambda j, i: (jnp.where(j < nj, i, ni - 1),
                                                    cb + jnp.minimum(j, nj - 1)))
    o_spec = pl.BlockSpec((tm, tn), lambda j, i: (row_tile(j, i), ob + jnp.maximum(j - 1, 0)))
    alias_in = [] if into is None else [into]
    n_in = len(xs) + len(ws) + len(side_in)
    kern = functools.partial(_staged_kernel, n_x=len(xs), n_w=len(ws), n_alias=len(alias_in),
                             side_steps=tuple(side_steps), nj=nj, slab=slab, epilogue=epilogue)
    outs = pl.pallas_call(
        kern,
        grid=(nj + 1, ni),
        in_specs=(x_specs + [w_spec] * len(ws) + side_specs
                  + [pl.BlockSpec(memory_space=pl.ANY)] * len(alias_in)),
        out_specs=[o_spec] + side_specs,
        out_shape=[out_struct] + side_shapes,
        scratch_shapes=[pltpu.VMEM((2, k, tn), BF16) for _ in ws],
        input_output_aliases={n_in: 0} if alias_in else {},
        compiler_params=_params(2),
        name=name,
    )(*xs, *ws, *side_in, *alias_in)
    return outs


def _store_epilogue(accs, o_ref):
    o_ref[...] = accs[0].astype(o_ref.dtype)


def _swiglu_epilogue(accs, o_ref):
    g, u = accs
    o_ref[...] = (jax.nn.silu(g) * u).astype(o_ref.dtype)


def _rope(t, cos, sin_signed):
    return t * cos + pltpu.roll(t, HEAD_DIM // 2, 1) * sin_signed


def _ffn(h, w_gate, w_up, w_down, side_casts=()):
    d_ff = w_gate.shape[1]
    tm, tn, tn_tail = 1024, 512, 256
    n_main = d_ff // tn * tn
    a, *copies = _staged_matmul([h], [w_gate, w_up], n=n_main, out_cols=d_ff, out_dtype=BF16,
                                epilogue=_swiglu_epilogue, name="ffn_gate_up", tm=tm, tn=tn,
                                side_casts=side_casts)
    if n_main < d_ff:
        a, = _staged_matmul([h], [w_gate, w_up], n=d_ff - n_main, col0=n_main, into=a,
                            out_dtype=BF16, epilogue=_swiglu_epilogue, name="ffn_gate_up_tail",
                            tm=2 * tm, tn=tn_tail)
    y, = _staged_matmul([a], [w_down], n=w_down.shape[1], out_dtype=F32,
                        epilogue=_store_epilogue, name="ffn_down", tm=512, tn=512)
    return y, copies


def _in_proj_pool(h, w, tm=1024, tn=512):
    return _staged_matmul([h], [w], n=POOL_WIDTH, out_dtype=F32, epilogue=_store_epilogue,
                          name="mix_in_proj_pool", tm=tm, tn=tn, col0=ATTN_WIDTH + 2 * KV_WIDTH)[0]


def _resident(shape):
    return pl.BlockSpec(shape, lambda t: (0,) * len(shape), pipeline_mode=pl.Buffered(1))


def _residual_qkv_kernel(x_ref, y_ref, gpost_ref, gnext_ref, w_ref, cos_ref, sin_ref,
                         xo_ref, h_ref, qkv_ref, *, weight):
    xn = x_ref[...] + weight * _rms(y_ref[...], gpost_ref[...])
    xo_ref[...] = xn
    h = _rms(xn, gnext_ref[...]).astype(BF16)
    h_ref[...] = h
    acc = jnp.dot(h, w_ref[...], preferred_element_type=F32)
    for c in range(acc.shape[1] // HEAD_DIM):
        cols = slice(c * HEAD_DIM, (c + 1) * HEAD_DIM)
        if c * HEAD_DIM < ATTN_WIDTH + KV_WIDTH:
            kind = 0 if c * HEAD_DIM < ATTN_WIDTH else 1
            val = _rope(acc[:, cols], cos_ref[kind], sin_ref[kind])
        else:
            val = acc[:, cols]
        qkv_ref[:, cols] = val.astype(qkv_ref.dtype)


def _residual_qkv(x, y, g_post, g_next, w_qkv, cos2, sin2, seq, weight, tm=256):
    m, d = x.shape
    n = ATTN_WIDTH + 2 * KV_WIDTH
    assert w_qkv.shape == (d, n) and w_qkv.dtype == BF16 and m % tm == 0 and seq % tm == 0
    row = lambda width: pl.BlockSpec((tm, width), lambda t: (t, 0))
    vec = pl.BlockSpec((1, d), lambda t: (0, 0))
    table = pl.BlockSpec((2, tm, HEAD_DIM), lambda t: (0, t % (seq // tm), 0))
    return pl.pallas_call(
        functools.partial(_residual_qkv_kernel, weight=weight),
        grid=(m // tm,),
        in_specs=[row(d), row(d), vec, vec, _resident((d, n)), table, table],
        out_specs=[row(d), row(d), row(n)],
        out_shape=[jax.ShapeDtypeStruct((m, d), F32), jax.ShapeDtypeStruct((m, d), BF16),
                   jax.ShapeDtypeStruct((m, n), BF16)],
        compiler_params=_params(1),
        name="residual_qkv",
    )(x, y, g_post, g_next, w_qkv, cos2, sin2)


def _out_proj_residual_kernel(a_ref, p_ref, x_ref, gpost_ref, gnext_ref, w_ref, xo_ref, h_ref):
    ka = a_ref.shape[1]
    y = (jnp.dot(a_ref[...], w_ref[0:ka, :], preferred_element_type=F32)
         + jnp.dot(p_ref[...], w_ref[ka:, :], preferred_element_type=F32))
    xn = x_ref[...] + _rms(y, gpost_ref[...])
    xo_ref[...] = xn
    h_ref[...] = _rms(xn, gnext_ref[...]).astype(h_ref.dtype)


def _out_proj_residual(a, pm, x, g_post, g_next, w_bf16, tm=256):
    m, d = x.shape
    k = a.shape[1] + pm.shape[1]
    assert w_bf16.shape == (k, d) and w_bf16.dtype == BF16 and m % tm == 0
    row = lambda width: pl.BlockSpec((tm, width), lambda t: (t, 0))
    vec = pl.BlockSpec((1, d), lambda t: (0, 0))
    return pl.pallas_call(
        _out_proj_residual_kernel,
        grid=(m // tm,),
        in_specs=[row(a.shape[1]), row(pm.shape[1]), row(d), vec, vec, _resident((k, d))],
        out_specs=[row(d), row(d)],
        out_shape=[jax.ShapeDtypeStruct((m, d), F32), jax.ShapeDtypeStruct((m, d), BF16)],
        compiler_params=_params(1),
        name="out_proj_residual",
    )(a, pm, x, g_post, g_next, w_bf16)


def _attn_kernel(sinks_ref, q_ref, k_ref, v_ref, o_ref, bias_s, *, seq):
    kvh = pl.program_id(1)
    g, blk = GQA_GROUP, BLOCK
    key = lax.broadcasted_iota(jnp.int32, (2 * blk, blk), 0)
    qry = lax.broadcasted_iota(jnp.int32, (2 * blk, blk), 1)
    cur_ok = (key >= blk) & (key - blk <= qry)
    prev_ok = (key < blk) & (key > qry)
    bias_s[0] = jnp.where(cur_ok, 0.0, -jnp.inf)
    bias_s[1] = jnp.where(cur_ok | prev_ok, 0.0, -jnp.inf)
    sinks = [jnp.full((1, blk), sinks_ref[kvh * g + c], F32) for c in range(g)]
    contract_last = (((1,), (1,)), ((), ()))
    contract_first = (((0,), (0,)), ((), ()))

    def body(n, carry):
        r0 = pl.multiple_of(n * blk, blk)
        rp = pl.multiple_of(jnp.maximum(n - 1, 0) * blk, blk)
        kw = jnp.concatenate([k_ref[pl.ds(rp, blk), :], k_ref[pl.ds(r0, blk), :]], axis=0)
        vw = jnp.concatenate([v_ref[pl.ds(rp, blk), :], v_ref[pl.ds(r0, blk), :]], axis=0)
        bias = bias_s[jnp.minimum(n, 1)]
        for c in range(g):
            qc = q_ref[pl.ds(r0, blk), c * HEAD_DIM:(c + 1) * HEAD_DIM]
            s = lax.dot_general(kw, qc, contract_last, preferred_element_type=F32) + bias
            m = jnp.maximum(jnp.max(s, axis=0, keepdims=True), sinks[c])
            e = jnp.exp(s - m)
            denom = jnp.sum(e, axis=0, keepdims=True) + jnp.exp(sinks[c] - m)
            p = (e * (1.0 / denom)).astype(BF16)
            o = lax.dot_general(p, vw, contract_first, preferred_element_type=F32)
            o_ref[pl.ds(r0, blk), c * HEAD_DIM:(c + 1) * HEAD_DIM] = o.astype(o_ref.dtype)
        return carry

    lax.fori_loop(0, seq // blk, body, 0, unroll=16)


def _attention(qkv, sinks, batch, seq):
    m = qkv.shape[0]
    qw = GQA_GROUP * HEAD_DIM
    k_blk0 = ATTN_WIDTH // HEAD_DIM
    v_blk0 = (ATTN_WIDTH + KV_WIDTH) // HEAD_DIM
    return pl.pallas_call(
        functools.partial(_attn_kernel, seq=seq),
        grid=(batch, N_KV_HEADS),
        in_specs=[pl.BlockSpec(memory_space=pltpu.SMEM),
                  pl.BlockSpec((seq, qw), lambda b, h: (b, h)),
                  pl.BlockSpec((seq, HEAD_DIM), lambda b, h: (b, k_blk0 + h)),
                  pl.BlockSpec((seq, HEAD_DIM), lambda b, h: (b, v_blk0 + h))],
        out_specs=pl.BlockSpec((seq, qw), lambda b, h: (b, h)),
        out_shape=jax.ShapeDtypeStruct((m, ATTN_WIDTH), BF16),
        scratch_shapes=[pltpu.VMEM((2, 2 * BLOCK, BLOCK), F32)],
        compiler_params=_params(2),
        name="swa_attention",
    )(sinks, qkv, qkv, qkv)


def _pool_kernel(p_ref, w_ref, sc_ref, o_ref, pad_s, y_s, *, seq, rows):
    grp = pl.program_id(1)
    gw = POOL_GROUP_WIDTH
    pad_s[0:POOL_HALO, :] = jnp.zeros((POOL_HALO, gw), F32)
    pad_s[POOL_HALO:, :] = p_ref[...]
    ext = rows + POOL_HALO
    lrow = lax.broadcasted_iota(jnp.int32, (rows, 128), 0)

    for gi, w in enumerate(POOL_WINDOWS):
        assert w & (w - 1) == 0 and w - 1 <= POOL_HALO

        @pl.when(grp == gi)
        def _(w=w):
            def body(r, carry):
                r0 = pl.multiple_of(r * rows, rows)
                cnt = jnp.minimum(lrow + (r0 + 1), w).astype(F32)
                for c in range(gw // 128):
                    x = pad_s[pl.ds(r0, ext), c * 128:(c + 1) * 128]
                    s = x
                    d = 1
                    while d < w:
                        s = s + pltpu.roll(s, d, 0)
                        d *= 2
                    y = s[POOL_HALO:] / cnt - x[POOL_HALO:]
                    y_s[pl.ds(r0, rows), c * 128:(c + 1) * 128] = y.astype(BF16)
                return carry

            lax.fori_loop(0, seq // rows, body, 0)

    out = jnp.dot(y_s[...], w_ref[0].astype(BF16), preferred_element_type=F32) * sc_ref[...]
    o_ref[...] = out.astype(o_ref.dtype)


def _pool(proj, pool_w, pool_scale, batch, seq, rows=128):
    m = proj.shape[0]
    gw = POOL_GROUP_WIDTH
    ng = len(POOL_WINDOWS)
    return pl.pallas_call(
        functools.partial(_pool_kernel, seq=seq, rows=rows),
        grid=(batch, ng),
        in_specs=[pl.BlockSpec((seq, gw), lambda b, g: (b, g)),
                  pl.BlockSpec((1, gw, gw), lambda b, g: (g, 0, 0)),
                  pl.BlockSpec((1, gw), lambda b, g: (0, g))],
        out_specs=pl.BlockSpec((seq, gw), lambda b, g: (b, g)),
        out_shape=jax.ShapeDtypeStruct((m, POOL_WIDTH), BF16),
        scratch_shapes=[pltpu.VMEM((seq + POOL_HALO, gw), F32), pltpu.VMEM((seq, gw), BF16)],
        compiler_params=_params(2),
        name="multiscale_pool",
    )(proj, pool_w, pool_scale)


def _rope_tables(seq):
    pos = np.arange(seq, dtype=np.float32)
    inv_freq = (np.float32(ROPE_THETA)
                ** (-np.arange(0, HEAD_DIM, 2, dtype=np.float32) / np.float32(HEAD_DIM)))
    ang = (pos[:, None] * inv_freq[None, :]).astype(np.float32)
    cos, sin = np.cos(ang).astype(np.float32), np.sin(ang).astype(np.float32)
    cos = np.concatenate([cos, cos], axis=-1)
    sin = np.concatenate([-sin, sin], axis=-1)
    scale = np.float32(HEAD_DIM ** -0.5)
    return jnp.asarray(np.stack([cos * scale, cos])), jnp.asarray(np.stack([sin * scale, sin]))


def kernel(x, ffn1_pre_g, ffn1_w_gate, ffn1_w_up, ffn1_w_down, ffn1_post_g, mix_pre_g, w_in, attn_sinks, pool_w, pool_scale, w_out, mix_post_g, ffn2_pre_g, ffn2_w_gate, ffn2_w_up, ffn2_w_down, ffn2_post_g):
    b, s, d = x.shape
    depth = ffn1_pre_g.shape[0]
    cos2, sin2 = _rope_tables(s)
    vec = lambda g: g.reshape(1, -1)
    xf = x.reshape(b * s, d)
    h = _prenorm(xf, vec(ffn1_pre_g[0]))
    for l in range(depth):
        side = [(w_in[l], SIDE_CAST_ROWS, ATTN_WIDTH + 2 * KV_WIDTH),
                (w_out[l], SIDE_CAST_ROWS, d)]
        y, (w_qkv, w_o) = _ffn(h, ffn1_w_gate[l], ffn1_w_up[l], ffn1_w_down[l], side)
        xf, h, qkv = _residual_qkv(xf, y, vec(ffn1_post_g[l]), vec(mix_pre_g[l]), w_qkv,
                                   cos2, sin2, s, FFN_RES_WEIGHT)
        a = _attention(qkv, attn_sinks[l], b, s)
        pm = _pool(_in_proj_pool(h, w_in[l]), pool_w[l], vec(pool_scale[l]), b, s)
        xf, h = _out_proj_residual(a, pm, xf, vec(mix_post_g[l]), vec(ffn2_pre_g[l]), w_o)

        y, _ = _ffn(h, ffn2_w_gate[l], ffn2_w_up[l], ffn2_w_down[l])
        g_next = vec(ffn1_pre_g[l + 1]) if l + 1 < depth else None
        xf, h = _residual(xf, y, vec(ffn2_post_g[l]), g_next, FFN_RES_WEIGHT)
    return xf.reshape(b, s, d)
```

```python
import functools

import jax
import jax.numpy as jnp
import numpy as np
from jax import lax
from jax.experimental import pallas as pl
from jax.experimental.pallas import tpu as pltpu

D_MODEL = 4096
HEAD_DIM = 128
N_HEADS = 16
N_KV_HEADS = 4
GQA_GROUP = N_HEADS // N_KV_HEADS
ATTN_WIDTH = N_HEADS * HEAD_DIM
KV_WIDTH = N_KV_HEADS * HEAD_DIM
BLOCK = 128
ROPE_THETA = 10000.0
POOL_WIDTH = D_MODEL - ATTN_WIDTH
POOL_WINDOWS = (2, 4, 8, 16)
POOL_GROUP_WIDTH = POOL_WIDTH // len(POOL_WINDOWS)
IN_PROJ_WIDTH = ATTN_WIDTH + 2 * KV_WIDTH + POOL_WIDTH
FFN_RES_WEIGHT = 0.5
RMS_EPS = 1e-6
POOL_HALO = 16

BF16 = jnp.bfloat16
F32 = jnp.float32
BF16_SUBLANES = 16
SIDE_CAST_ROWS = 32

VMEM_LIMIT_BYTES = 62 * 1024 * 1024


def _params(n_axes):
    return pltpu.CompilerParams(
        dimension_semantics=("arbitrary",) * n_axes,
        vmem_limit_bytes=VMEM_LIMIT_BYTES,
    )


def _rms(x, g):
    ms = jnp.mean(x * x, axis=-1, keepdims=True)
    return x * lax.rsqrt(ms + RMS_EPS) * g


def _norm_kernel(x_ref, g_ref, h_ref):
    h_ref[...] = _rms(x_ref[...], g_ref[...]).astype(h_ref.dtype)


def _prenorm(x, g, tm=512):
    m, d = x.shape
    return pl.pallas_call(
        _norm_kernel,
        grid=(m // tm,),
        in_specs=[pl.BlockSpec((tm, d), lambda i: (i, 0)),
                  pl.BlockSpec((1, d), lambda i: (0, 0))],
        out_specs=pl.BlockSpec((tm, d), lambda i: (i, 0)),
        out_shape=jax.ShapeDtypeStruct((m, d), BF16),
        compiler_params=_params(1),
        name="prenorm",
    )(x, g)


def _residual_kernel(x_ref, y_ref, gpost_ref, *rest, weight, with_next):
    if with_next:
        gnext_ref, xo_ref, h_ref = rest
    else:
        (xo_ref,) = rest
    r = _rms(y_ref[...], gpost_ref[...])
    if weight != 1.0:
        r = weight * r
    xn = x_ref[...] + r
    xo_ref[...] = xn
    if with_next:
        h_ref[...] = _rms(xn, gnext_ref[...]).astype(h_ref.dtype)


def _residual(x, y, g_post, g_next, weight, tm=256):
    m, d = x.shape
    with_next = g_next is not None
    row = pl.BlockSpec((tm, d), lambda i: (i, 0))
    vec = pl.BlockSpec((1, d), lambda i: (0, 0))
    in_specs = [row, row, vec] + ([vec] if with_next else [])
    args = (x, y, g_post) + ((g_next,) if with_next else ())
    out_specs = [row] + ([row] if with_next else [])
    out_shape = [jax.ShapeDtypeStruct((m, d), F32)] + (
        [jax.ShapeDtypeStruct((m, d), BF16)] if with_next else [])
    outs = pl.pallas_call(
        functools.partial(_residual_kernel, weight=weight, with_next=with_next),
        grid=(m // tm,),
        in_specs=in_specs,
        out_specs=out_specs,
        out_shape=out_shape,
        compiler_params=_params(1),
        name="residual_norm",
    )(*args)
    return (outs[0], outs[1]) if with_next else (outs[0], None)


def _staged_kernel(*refs, n_x, n_w, n_first, n_alias, side_steps, nj, slab, epilogue):
    n_side = len(side_steps)
    n_in = n_x + n_w
    x_refs = refs[:n_x]
    w_refs = refs[n_x:n_in]
    first_refs = refs[n_in:n_in + n_first]
    n_in += n_first
    side_in = refs[n_in:n_in + n_side]
    n_in += n_side + n_alias
    o_ref = refs[n_in]
    side_out = refs[n_in + 1:n_in + 1 + n_side]
    w_s = refs[n_in + 1 + n_side:]
    j = pl.program_id(0)
    i = pl.program_id(1)
    lead = 0 if n_first else 1
    staged, multiplied = j + 1 - lead, j - lead

    for src, dst, steps in zip(side_in, side_out, side_steps):
        @pl.when(j * pl.num_programs(1) + i < steps)
        def _(src=src, dst=dst):
            dst[...] = src[...].astype(dst.dtype)

    if n_first:
        @pl.when((j == 0) & (i == 0))
        def _():
            for f_ref, s in zip(first_refs, w_s):
                s[0] = f_ref[...]

    @pl.when(staged < nj)
    def _():
        rows = pl.ds(pl.multiple_of(i * slab, slab), slab)
        for w_ref, s in zip(w_refs, w_s):
            s[staged % 2, rows, :] = w_ref[...].astype(BF16)

    @pl.when(multiplied >= 0)
    def _():
        slot = multiplied % 2
        accs = []
        for s in w_s:
            acc, k0 = None, 0
            for x_ref in x_refs:
                kx = x_ref.shape[1]
                part = jnp.dot(x_ref[...], s[slot, k0:k0 + kx, :], preferred_element_type=F32)
                acc = part if acc is None else acc + part
                k0 += kx
            accs.append(acc)
        epilogue(accs, o_ref)


def _staged_matmul(xs, ws, *, n, out_dtype, epilogue, name, tm, tn, col0=0, out_cols=None,
                   into=None, side_casts=(), first=()):
    m = xs[0].shape[0]
    k = sum(x.shape[1] for x in xs)
    assert all(w.shape[0] == k for w in ws) and m % tm == 0 and col0 % tn == 0 and n % tn == 0
    assert not first or (len(first) == len(ws) and all(
        f.shape == (k, tn) and f.dtype == BF16 for f in first))
    ni, nj = m // tm, n // tn
    assert k % (ni * BF16_SUBLANES) == 0
    slab = k // ni
    cb = col0 // tn
    lead = 0 if first else 1
    ob = cb if (out_cols or into is not None) else 0
    out_struct = jax.ShapeDtypeStruct((m, out_cols or n), out_dtype) if into is None else (
        jax.ShapeDtypeStruct(into.shape, into.dtype))
    side_in, side_specs, side_shapes, side_steps = [], [], [], []
    for arr, rows, cols in side_casts:
        steps = arr.shape[0] // rows
        assert arr.shape[0] % rows == 0 and rows % BF16_SUBLANES == 0 and steps <= (nj + lead) * ni
        side_in.append(arr)
        side_specs.append(pl.BlockSpec(
            (rows, cols), lambda j, i, steps=steps: (jnp.minimum(j * ni + i, steps - 1), 0)))
        side_shapes.append(jax.ShapeDtypeStruct((arr.shape[0], cols), BF16))
        side_steps.append(steps)
    row_tile = lambda j, i: jnp.where(j < lead, 0, i)
    staged = lambda j: j + 1 - lead
    x_specs = [pl.BlockSpec((tm, x.shape[1]), lambda j, i: (row_tile(j, i), 0)) for x in xs]
    w_spec = pl.BlockSpec((slab, tn), lambda j, i: (jnp.where(staged(j) < nj, i, ni - 1),
                                                    cb + jnp.minimum(staged(j), nj - 1)))
    o_spec = pl.BlockSpec((tm, tn), lambda j, i: (row_tile(j, i), ob + jnp.maximum(j - lead, 0)))
    first_specs = [pl.BlockSpec((k, tn), lambda j, i: (0, 0), pipeline_mode=pl.Buffered(1))
                   for _ in first]
    alias_in = [] if into is None else [into]
    n_in = len(xs) + len(ws) + len(first) + len(side_in)
    kern = functools.partial(_staged_kernel, n_x=len(xs), n_w=len(ws), n_first=len(first),
                             n_alias=len(alias_in), side_steps=tuple(side_steps), nj=nj,
                             slab=slab, epilogue=epilogue)
    outs = pl.pallas_call(
        kern,
        grid=(nj + lead, ni),
        in_specs=(x_specs + [w_spec] * len(ws) + first_specs + side_specs
                  + [pl.BlockSpec(memory_space=pl.ANY)] * len(alias_in)),
        out_specs=[o_spec] + side_specs,
        out_shape=[out_struct] + side_shapes,
        scratch_shapes=[pltpu.VMEM((2, k, tn), BF16) for _ in ws],
        input_output_aliases={n_in: 0} if alias_in else {},
        compiler_params=_params(2),
        name=name,
    )(*xs, *ws, *first, *side_in, *alias_in)
    return outs


def _store_epilogue(accs, o_ref):
    o_ref[...] = accs[0].astype(o_ref.dtype)


def _swiglu_epilogue(accs, o_ref):
    g, u = accs
    o_ref[...] = (jax.nn.silu(g) * u).astype(o_ref.dtype)


def _rope(t, cos, sin_signed):
    return t * cos + pltpu.roll(t, HEAD_DIM // 2, 1) * sin_signed


def _ffn(h, w_gate, w_up, w_down, side_casts=()):
    d_ff = w_gate.shape[1]
    tm, tn, tn_tail = 1024, 512, 256
    tn_down = 512
    n_main = d_ff // tn * tn
    a, *copies, wd_first = _staged_matmul(
        [h], [w_gate, w_up], n=n_main, out_cols=d_ff, out_dtype=BF16, epilogue=_swiglu_epilogue,
        name="ffn_gate_up", tm=tm, tn=tn,
        side_casts=list(side_casts) + [(w_down, 2 * SIDE_CAST_ROWS, tn_down)])
    if n_main < d_ff:
        a, = _staged_matmul([h], [w_gate, w_up], n=d_ff - n_main, col0=n_main, into=a,
                            out_dtype=BF16, epilogue=_swiglu_epilogue, name="ffn_gate_up_tail",
                            tm=2 * tm, tn=tn_tail)
    y, = _staged_matmul([a], [w_down], n=w_down.shape[1], out_dtype=F32, first=[wd_first],
                        epilogue=_store_epilogue, name="ffn_down", tm=512, tn=tn_down)
    return y, copies


def _in_proj_pool(h, w, tm=1024, tn=512):
    return _staged_matmul([h], [w], n=POOL_WIDTH, out_dtype=F32, epilogue=_store_epilogue,
                          name="mix_in_proj_pool", tm=tm, tn=tn, col0=ATTN_WIDTH + 2 * KV_WIDTH)[0]


def _resident(shape):
    return pl.BlockSpec(shape, lambda t: (0,) * len(shape), pipeline_mode=pl.Buffered(1))


def _residual_qkv_kernel(x_ref, y_ref, gpost_ref, gnext_ref, w_ref, cos_ref, sin_ref,
                         xo_ref, h_ref, qkv_ref, *, weight):
    xn = x_ref[...] + weight * _rms(y_ref[...], gpost_ref[...])
    xo_ref[...] = xn
    h = _rms(xn, gnext_ref[...]).astype(BF16)
    h_ref[...] = h
    acc = jnp.dot(h, w_ref[...], preferred_element_type=F32)
    for c in range(acc.shape[1] // HEAD_DIM):
        cols = slice(c * HEAD_DIM, (c + 1) * HEAD_DIM)
        if c * HEAD_DIM < ATTN_WIDTH + KV_WIDTH:
            kind = 0 if c * HEAD_DIM < ATTN_WIDTH else 1
            val = _rope(acc[:, cols], cos_ref[kind], sin_ref[kind])
        else:
            val = acc[:, cols]
        qkv_ref[:, cols] = val.astype(qkv_ref.dtype)


def _residual_qkv(x, y, g_post, g_next, w_qkv, cos2, sin2, seq, weight, tm=256):
    m, d = x.shape
    n = ATTN_WIDTH + 2 * KV_WIDTH
    assert w_qkv.shape == (d, n) and w_qkv.dtype == BF16 and m % tm == 0 and seq % tm == 0
    row = lambda width: pl.BlockSpec((tm, width), lambda t: (t, 0))
    vec = pl.BlockSpec((1, d), lambda t: (0, 0))
    table = pl.BlockSpec((2, tm, HEAD_DIM), lambda t: (0, t % (seq // tm), 0))
    return pl.pallas_call(
        functools.partial(_residual_qkv_kernel, weight=weight),
        grid=(m // tm,),
        in_specs=[row(d), row(d), vec, vec, _resident((d, n)), table, table],
        out_specs=[row(d), row(d), row(n)],
        out_shape=[jax.ShapeDtypeStruct((m, d), F32), jax.ShapeDtypeStruct((m, d), BF16),
                   jax.ShapeDtypeStruct((m, n), BF16)],
        compiler_params=_params(1),
        name="residual_qkv",
    )(x, y, g_post, g_next, w_qkv, cos2, sin2)


def _out_proj_residual_kernel(a_ref, p_ref, x_ref, gpost_ref, gnext_ref, w_ref, xo_ref, h_ref):
    ka = a_ref.shape[1]
    y = (jnp.dot(a_ref[...], w_ref[0:ka, :], preferred_element_type=F32)
         + jnp.dot(p_ref[...], w_ref[ka:, :], preferred_element_type=F32))
    xn = x_ref[...] + _rms(y, gpost_ref[...])
    xo_ref[...] = xn
    h_ref[...] = _rms(xn, gnext_ref[...]).astype(h_ref.dtype)


def _out_proj_residual(a, pm, x, g_post, g_next, w_bf16, tm=256):
    m, d = x.shape
    k = a.shape[1] + pm.shape[1]
    assert w_bf16.shape == (k, d) and w_bf16.dtype == BF16 and m % tm == 0
    row = lambda width: pl.BlockSpec((tm, width), lambda t: (t, 0))
    vec = pl.BlockSpec((1, d), lambda t: (0, 0))
    return pl.pallas_call(
        _out_proj_residual_kernel,
        grid=(m // tm,),
        in_specs=[row(a.shape[1]), row(pm.shape[1]), row(d), vec, vec, _resident((k, d))],
        out_specs=[row(d), row(d)],
        out_shape=[jax.ShapeDtypeStruct((m, d), F32), jax.ShapeDtypeStruct((m, d), BF16)],
        compiler_params=_params(1),
        name="out_proj_residual",
    )(a, pm, x, g_post, g_next, w_bf16)


def _attn_kernel(sinks_ref, q_ref, k_ref, v_ref, o_ref, bias_s, *, seq):
    kvh = pl.program_id(1)
    g, blk = GQA_GROUP, BLOCK
    key = lax.broadcasted_iota(jnp.int32, (2 * blk, blk), 0)
    qry = lax.broadcasted_iota(jnp.int32, (2 * blk, blk), 1)
    cur_ok = (key >= blk) & (key - blk <= qry)
    prev_ok = (key < blk) & (key > qry)
    bias_s[0] = jnp.where(cur_ok, 0.0, -jnp.inf)
    bias_s[1] = jnp.where(cur_ok | prev_ok, 0.0, -jnp.inf)
    sinks = [jnp.full((1, blk), sinks_ref[kvh * g + c], F32) for c in range(g)]
    contract_last = (((1,), (1,)), ((), ()))
    contract_first = (((0,), (0,)), ((), ()))

    def body(n, carry):
        r0 = pl.multiple_of(n * blk, blk)
        rp = pl.multiple_of(jnp.maximum(n - 1, 0) * blk, blk)
        kw = jnp.concatenate([k_ref[pl.ds(rp, blk), :], k_ref[pl.ds(r0, blk), :]], axis=0)
        vw = jnp.concatenate([v_ref[pl.ds(rp, blk), :], v_ref[pl.ds(r0, blk), :]], axis=0)
        bias = bias_s[jnp.minimum(n, 1)]
        for c in range(g):
            qc = q_ref[pl.ds(r0, blk), c * HEAD_DIM:(c + 1) * HEAD_DIM]
            s = lax.dot_general(kw, qc, contract_last, preferred_element_type=F32) + bias
            m = jnp.maximum(jnp.max(s, axis=0, keepdims=True), sinks[c])
            e = jnp.exp(s - m)
            denom = jnp.sum(e, axis=0, keepdims=True) + jnp.exp(sinks[c] - m)
            p = (e * (1.0 / denom)).astype(BF16)
            o = lax.dot_general(p, vw, contract_first, preferred_element_type=F32)
            o_ref[pl.ds(r0, blk), c * HEAD_DIM:(c + 1) * HEAD_DIM] = o.astype(o_ref.dtype)
        return carry

    lax.fori_loop(0, seq // blk, body, 0, unroll=16)


def _attention(qkv, sinks, batch, seq):
    m = qkv.shape[0]
    qw = GQA_GROUP * HEAD_DIM
    k_blk0 = ATTN_WIDTH // HEAD_DIM
    v_blk0 = (ATTN_WIDTH + KV_WIDTH) // HEAD_DIM
    return pl.pallas_call(
        functools.partial(_attn_kernel, seq=seq),
        grid=(batch, N_KV_HEADS),
        in_specs=[pl.BlockSpec(memory_space=pltpu.SMEM),
                  pl.BlockSpec((seq, qw), lambda b, h: (b, h)),
                  pl.BlockSpec((seq, HEAD_DIM), lambda b, h: (b, k_blk0 + h)),
                  pl.BlockSpec((seq, HEAD_DIM), lambda b, h: (b, v_blk0 + h))],
        out_specs=pl.BlockSpec((seq, qw), lambda b, h: (b, h)),
        out_shape=jax.ShapeDtypeStruct((m, ATTN_WIDTH), BF16),
        scratch_shapes=[pltpu.VMEM((2, 2 * BLOCK, BLOCK), F32)],
        compiler_params=_params(2),
        name="swa_attention",
    )(sinks, qkv, qkv, qkv)


def _pool_kernel(p_ref, w_ref, sc_ref, o_ref, pad_s, y_s, *, seq, rows):
    grp = pl.program_id(1)
    gw = POOL_GROUP_WIDTH
    pad_s[0:POOL_HALO, :] = jnp.zeros((POOL_HALO, gw), F32)
    pad_s[POOL_HALO:, :] = p_ref[...]
    ext = rows + POOL_HALO
    lrow = lax.broadcasted_iota(jnp.int32, (rows, 128), 0)

    for gi, w in enumerate(POOL_WINDOWS):
        assert w & (w - 1) == 0 and w - 1 <= POOL_HALO

        @pl.when(grp == gi)
        def _(w=w):
            def body(r, carry):
                r0 = pl.multiple_of(r * rows, rows)
                cnt = jnp.minimum(lrow + (r0 + 1), w).astype(F32)
                for c in range(gw // 128):
                    x = pad_s[pl.ds(r0, ext), c * 128:(c + 1) * 128]
                    s = x
                    d = 1
                    while d < w:
                        s = s + pltpu.roll(s, d, 0)
                        d *= 2
                    y = s[POOL_HALO:] / cnt - x[POOL_HALO:]
                    y_s[pl.ds(r0, rows), c * 128:(c + 1) * 128] = y.astype(BF16)
                return carry

            lax.fori_loop(0, seq // rows, body, 0)

    out = jnp.dot(y_s[...], w_ref[0].astype(BF16), preferred_element_type=F32) * sc_ref[...]
    o_ref[...] = out.astype(o_ref.dtype)


def _pool(proj, pool_w, pool_scale, batch, seq, rows=128):
    m = proj.shape[0]
    gw = POOL_GROUP_WIDTH
    ng = len(POOL_WINDOWS)
    return pl.pallas_call(
        functools.partial(_pool_kernel, seq=seq, rows=rows),
        grid=(batch, ng),
        in_specs=[pl.BlockSpec((seq, gw), lambda b, g: (b, g)),
                  pl.BlockSpec((1, gw, gw), lambda b, g: (g, 0, 0)),
                  pl.BlockSpec((1, gw), lambda b, g: (0, g))],
        out_specs=pl.BlockSpec((seq, gw), lambda b, g: (b, g)),
        out_shape=jax.ShapeDtypeStruct((m, POOL_WIDTH), BF16),
        scratch_shapes=[pltpu.VMEM((seq + POOL_HALO, gw), F32), pltpu.VMEM((seq, gw), BF16)],
        compiler_params=_params(2),
        name="multiscale_pool",
    )(proj, pool_w, pool_scale)


def _rope_tables(seq):
    pos = np.arange(seq, dtype=np.float32)
    inv_freq = (np.float32(ROPE_THETA)
                ** (-np.arange(0, HEAD_DIM, 2, dtype=np.float32) / np.float32(HEAD_DIM)))
    ang = (pos[:, None] * inv_freq[None, :]).astype(np.float32)
    cos, sin = np.cos(ang).astype(np.float32), np.sin(ang).astype(np.float32)
    cos = np.concatenate([cos, cos], axis=-1)
    sin = np.concatenate([-sin, sin], axis=-1)
    scale = np.float32(HEAD_DIM ** -0.5)
    return jnp.asarray(np.stack([cos * scale, cos])), jnp.asarray(np.stack([sin * scale, sin]))


def kernel(x, ffn1_pre_g, ffn1_w_gate, ffn1_w_up, ffn1_w_down, ffn1_post_g, mix_pre_g, w_in, attn_sinks, pool_w, pool_scale, w_out, mix_post_g, ffn2_pre_g, ffn2_w_gate, ffn2_w_up, ffn2_w_down, ffn2_post_g):
    b, s, d = x.shape
    depth = ffn1_pre_g.shape[0]
    cos2, sin2 = _rope_tables(s)
    vec = lambda g: g.reshape(1, -1)
    xf = x.reshape(b * s, d)
    h = _prenorm(xf, vec(ffn1_pre_g[0]))
    for l in range(depth):
        side = [(w_in[l], SIDE_CAST_ROWS, ATTN_WIDTH + 2 * KV_WIDTH),
                (w_out[l], SIDE_CAST_ROWS, d)]
        y, (w_qkv, w_o) = _ffn(h, ffn1_w_gate[l], ffn1_w_up[l], ffn1_w_down[l], side)
        xf, h, qkv = _residual_qkv(xf, y, vec(ffn1_post_g[l]), vec(mix_pre_g[l]), w_qkv,
                                   cos2, sin2, s, FFN_RES_WEIGHT)
        a = _attention(qkv, attn_sinks[l], b, s)
        pm = _pool(_in_proj_pool(h, w_in[l]), pool_w[l], vec(pool_scale[l]), b, s)
        xf, h = _out_proj_residual(a, pm, xf, vec(mix_post_g[l]), vec(ffn2_pre_g[l]), w_o)

        y, _ = _ffn(h, ffn2_w_gate[l], ffn2_w_up[l], ffn2_w_down[l])
        g_next = vec(ffn1_pre_g[l + 1]) if l + 1 < depth else None
        xf, h = _residual(xf, y, vec(ffn2_post_g[l]), g_next, FFN_RES_WEIGHT)
    return xf.reshape(b, s, d)
```

```python
import functools

import jax
import jax.numpy as jnp
import numpy as np
from jax import lax
from jax.experimental import pallas as pl
from jax.experimental.pallas import tpu as pltpu

D_MODEL = 4096
HEAD_DIM = 128
N_HEADS = 16
N_KV_HEADS = 4
GQA_GROUP = N_HEADS // N_KV_HEADS
ATTN_WIDTH = N_HEADS * HEAD_DIM
KV_WIDTH = N_KV_HEADS * HEAD_DIM
BLOCK = 128
ROPE_THETA = 10000.0
POOL_WIDTH = D_MODEL - ATTN_WIDTH
POOL_WINDOWS = (2, 4, 8, 16)
POOL_GROUP_WIDTH = POOL_WIDTH // len(POOL_WINDOWS)
IN_PROJ_WIDTH = ATTN_WIDTH + 2 * KV_WIDTH + POOL_WIDTH
FFN_RES_WEIGHT = 0.5
RMS_EPS = 1e-6
POOL_HALO = 16

BF16 = jnp.bfloat16
F32 = jnp.float32
BF16_SUBLANES = 16
SIDE_CAST_ROWS = 32

VMEM_LIMIT_BYTES = 62 * 1024 * 1024


def _params(n_axes):
    return pltpu.CompilerParams(
        dimension_semantics=("arbitrary",) * n_axes,
        vmem_limit_bytes=VMEM_LIMIT_BYTES,
    )


def _rms(x, g):
    ms = jnp.mean(x * x, axis=-1, keepdims=True)
    return x * lax.rsqrt(ms + RMS_EPS) * g


def _norm_kernel(x_ref, g_ref, h_ref):
    h_ref[...] = _rms(x_ref[...], g_ref[...]).astype(h_ref.dtype)


def _prenorm(x, g, tm=256):
    m, d = x.shape
    return pl.pallas_call(
        _norm_kernel,
        grid=(m // tm,),
        in_specs=[pl.BlockSpec((tm, d), lambda i: (i, 0)),
                  pl.BlockSpec((1, d), lambda i: (0, 0))],
        out_specs=pl.BlockSpec((tm, d), lambda i: (i, 0)),
        out_shape=jax.ShapeDtypeStruct((m, d), BF16),
        compiler_params=_params(1),
        name="prenorm",
    )(x, g)


def _residual_kernel(x_ref, y_ref, gpost_ref, *rest, weight, with_next):
    if with_next:
        gnext_ref, xo_ref, h_ref = rest
    else:
        (xo_ref,) = rest
    r = _rms(y_ref[...], gpost_ref[...])
    if weight != 1.0:
        r = weight * r
    xn = x_ref[...] + r
    xo_ref[...] = xn
    if with_next:
        h_ref[...] = _rms(xn, gnext_ref[...]).astype(h_ref.dtype)


def _residual(x, y, g_post, g_next, weight, tm=128):
    m, d = x.shape
    with_next = g_next is not None
    row = pl.BlockSpec((tm, d), lambda i: (i, 0))
    vec = pl.BlockSpec((1, d), lambda i: (0, 0))
    in_specs = [row, row, vec] + ([vec] if with_next else [])
    args = (x, y, g_post) + ((g_next,) if with_next else ())
    out_specs = [row] + ([row] if with_next else [])
    out_shape = [jax.ShapeDtypeStruct((m, d), F32)] + (
        [jax.ShapeDtypeStruct((m, d), BF16)] if with_next else [])
    outs = pl.pallas_call(
        functools.partial(_residual_kernel, weight=weight, with_next=with_next),
        grid=(m // tm,),
        in_specs=in_specs,
        out_specs=out_specs,
        out_shape=out_shape,
        compiler_params=_params(1),
        name="residual_norm",
    )(*args)
    return (outs[0], outs[1]) if with_next else (outs[0], None)


def _staged_kernel(*refs, n_x, n_w, n_alias, side_steps, nj, slab, epilogue):
    n_side = len(side_steps)
    n_in = n_x + n_w
    x_refs = refs[:n_x]
    w_refs = refs[n_x:n_in]
    side_in = refs[n_in:n_in + n_side]
    n_in += n_side + n_alias
    o_ref = refs[n_in]
    side_out = refs[n_in + 1:n_in + 1 + n_side]
    w_s = refs[n_in + 1 + n_side:]
    j = pl.program_id(0)
    i = pl.program_id(1)

    for src, dst, steps in zip(side_in, side_out, side_steps):
        @pl.when(j * pl.num_programs(1) + i < steps)
        def _(src=src, dst=dst):
            dst[...] = src[...].astype(dst.dtype)

    @pl.when(j < nj)
    def _():
        rows = pl.ds(pl.multiple_of(i * slab, slab), slab)
        for w_ref, s in zip(w_refs, w_s):
            s[j % 2, rows, :] = w_ref[...].astype(BF16)

    @pl.when(j > 0)
    def _():
        slot = (j + 1) % 2
        accs = []
        for s in w_s:
            acc, k0 = None, 0
            for x_ref in x_refs:
                kx = x_ref.shape[1]
                part = jnp.dot(x_ref[...], s[slot, k0:k0 + kx, :], preferred_element_type=F32)
                acc = part if acc is None else acc + part
                k0 += kx
            accs.append(acc)
        epilogue(accs, o_ref)


def _staged_matmul(xs, ws, *, n, out_dtype, epilogue, name, tm, tn, col0=0, out_cols=None,
                   into=None, side_casts=()):
    m = xs[0].shape[0]
    k = sum(x.shape[1] for x in xs)
    assert all(w.shape[0] == k for w in ws) and m % tm == 0 and col0 % tn == 0 and n % tn == 0
    ni, nj = m // tm, n // tn
    assert k % (ni * BF16_SUBLANES) == 0
    slab = k // ni
    cb = col0 // tn
    ob = cb if (out_cols or into is not None) else 0
    out_struct = jax.ShapeDtypeStruct((m, out_cols or n), out_dtype) if into is None else (
        jax.ShapeDtypeStruct(into.shape, into.dtype))
    side_in, side_specs, side_shapes, side_steps = [], [], [], []
    for arr, rows, cols in side_casts:
        steps = arr.shape[0] // rows
        assert arr.shape[0] % rows == 0 and rows % BF16_SUBLANES == 0 and steps <= (nj + 1) * ni
        side_in.append(arr)
        side_specs.append(pl.BlockSpec(
            (rows, cols), lambda j, i, steps=steps: (jnp.minimum(j * ni + i, steps - 1), 0)))
        side_shapes.append(jax.ShapeDtypeStruct((arr.shape[0], cols), BF16))
        side_steps.append(steps)
    row_tile = lambda j, i: jnp.where(j == 0, 0, i)
    x_specs = [pl.BlockSpec((tm, x.shape[1]), lambda j, i: (row_tile(j, i), 0)) for x in xs]
    w_spec = pl.BlockSpec((slab, tn), lambda j, i: (jnp.where(j < nj, i, ni - 1),
                                                    cb + jnp.minimum(j, nj - 1)))
    o_spec = pl.BlockSpec((tm, tn), lambda j, i: (row_tile(j, i), ob + jnp.maximum(j - 1, 0)))
    alias_in = [] if into is None else [into]
    n_in = len(xs) + len(ws) + len(side_in)
    kern = functools.partial(_staged_kernel, n_x=len(xs), n_w=len(ws), n_alias=len(alias_in),
                             side_steps=tuple(side_steps), nj=nj, slab=slab, epilogue=epilogue)
    outs = pl.pallas_call(
        kern,
        grid=(nj + 1, ni),
        in_specs=(x_specs + [w_spec] * len(ws) + side_specs
                  + [pl.BlockSpec(memory_space=pl.ANY)] * len(alias_in)),
        out_specs=[o_spec] + side_specs,
        out_shape=[out_struct] + side_shapes,
        scratch_shapes=[pltpu.VMEM((2, k, tn), BF16) for _ in ws],
        input_output_aliases={n_in: 0} if alias_in else {},
        compiler_params=_params(2),
        name=name,
    )(*xs, *ws, *side_in, *alias_in)
    return outs


def _store_epilogue(accs, o_ref):
    o_ref[...] = accs[0].astype(o_ref.dtype)


def _swiglu_epilogue(accs, o_ref):
    g, u = accs
    o_ref[...] = (jax.nn.silu(g) * u).astype(o_ref.dtype)


def _rope(t, cos, sin_signed):
    return t * cos + pltpu.roll(t, HEAD_DIM // 2, 1) * sin_signed


def _ffn(h, w_gate, w_up, w_down, side_casts=()):
    d_ff = w_gate.shape[1]
    tm, tn, tn_tail = 1024, 512, 256
    n_main = d_ff // tn * tn
    a, *copies = _staged_matmul([h], [w_gate, w_up], n=n_main, out_cols=d_ff, out_dtype=BF16,
                                epilogue=_swiglu_epilogue, name="ffn_gate_up", tm=tm, tn=tn,
                                side_casts=side_casts)
    if n_main < d_ff:
        a, = _staged_matmul([h], [w_gate, w_up], n=d_ff - n_main, col0=n_main, into=a,
                            out_dtype=BF16, epilogue=_swiglu_epilogue, name="ffn_gate_up_tail",
                            tm=2 * tm, tn=tn_tail)
    y, = _staged_matmul([a], [w_down], n=w_down.shape[1], out_dtype=F32,
                        epilogue=_store_epilogue, name="ffn_down", tm=512, tn=512)
    return y, copies


def _in_proj_pool(h, w, tm=1024, tn=1024):
    return _staged_matmul([h], [w], n=POOL_WIDTH, out_dtype=F32, epilogue=_store_epilogue,
                          name="mix_in_proj_pool", tm=tm, tn=tn, col0=ATTN_WIDTH + 2 * KV_WIDTH)[0]


def _resident(shape):
    return pl.BlockSpec(shape, lambda t: (0,) * len(shape), pipeline_mode=pl.Buffered(1))


def _residual_qkv_kernel(x_ref, y_ref, gpost_ref, gnext_ref, w_ref, cos_ref, sin_ref,
                         xo_ref, h_ref, qkv_ref, *, weight):
    xn = x_ref[...] + weight * _rms(y_ref[...], gpost_ref[...])
    xo_ref[...] = xn
    h = _rms(xn, gnext_ref[...]).astype(BF16)
    h_ref[...] = h
    acc = jnp.dot(h, w_ref[...], preferred_element_type=F32)
    for c in range(acc.shape[1] // HEAD_DIM):
        cols = slice(c * HEAD_DIM, (c + 1) * HEAD_DIM)
        if c * HEAD_DIM < ATTN_WIDTH + KV_WIDTH:
            kind = 0 if c * HEAD_DIM < ATTN_WIDTH else 1
            val = _rope(acc[:, cols], cos_ref[kind], sin_ref[kind])
        else:
            val = acc[:, cols]
        qkv_ref[:, cols] = val.astype(qkv_ref.dtype)


def _residual_qkv(x, y, g_post, g_next, w_qkv, cos2, sin2, seq, weight, tm=256):
    m, d = x.shape
    n = ATTN_WIDTH + 2 * KV_WIDTH
    assert w_qkv.shape == (d, n) and w_qkv.dtype == BF16 and m % tm == 0 and seq % tm == 0
    row = lambda width: pl.BlockSpec((tm, width), lambda t: (t, 0))
    vec = pl.BlockSpec((1, d), lambda t: (0, 0))
    table = pl.BlockSpec((2, tm, HEAD_DIM), lambda t: (0, t % (seq // tm), 0))
    return pl.pallas_call(
        functools.partial(_residual_qkv_kernel, weight=weight),
        grid=(m // tm,),
        in_specs=[row(d), row(d), vec, vec, _resident((d, n)), table, table],
        out_specs=[row(d), row(d), row(n)],
        out_shape=[jax.ShapeDtypeStruct((m, d), F32), jax.ShapeDtypeStruct((m, d), BF16),
                   jax.ShapeDtypeStruct((m, n), BF16)],
        compiler_params=_params(1),
        name="residual_qkv",
    )(x, y, g_post, g_next, w_qkv, cos2, sin2)


def _out_proj_residual_kernel(a_ref, p_ref, x_ref, gpost_ref, gnext_ref, w_ref, xo_ref, h_ref):
    ka = a_ref.shape[1]
    y = (jnp.dot(a_ref[...], w_ref[0:ka, :], preferred_element_type=F32)
         + jnp.dot(p_ref[...], w_ref[ka:, :], preferred_element_type=F32))
    xn = x_ref[...] + _rms(y, gpost_ref[...])
    xo_ref[...] = xn
    h_ref[...] = _rms(xn, gnext_ref[...]).astype(h_ref.dtype)


def _out_proj_residual(a, pm, x, g_post, g_next, w_bf16, tm=256):
    m, d = x.shape
    k = a.shape[1] + pm.shape[1]
    assert w_bf16.shape == (k, d) and w_bf16.dtype == BF16 and m % tm == 0
    row = lambda width: pl.BlockSpec((tm, width), lambda t: (t, 0))
    vec = pl.BlockSpec((1, d), lambda t: (0, 0))
    return pl.pallas_call(
        _out_proj_residual_kernel,
        grid=(m // tm,),
        in_specs=[row(a.shape[1]), row(pm.shape[1]), row(d), vec, vec, _resident((k, d))],
        out_specs=[row(d), row(d)],
        out_shape=[jax.ShapeDtypeStruct((m, d), F32), jax.ShapeDtypeStruct((m, d), BF16)],
        compiler_params=_params(1),
        name="out_proj_residual",
    )(a, pm, x, g_post, g_next, w_bf16)


def _attn_kernel(sinks_ref, q_ref, k_ref, v_ref, o_ref, bias_s, *, seq):
    kvh = pl.program_id(1)
    g, blk = GQA_GROUP, BLOCK
    key = lax.broadcasted_iota(jnp.int32, (2 * blk, blk), 0)
    qry = lax.broadcasted_iota(jnp.int32, (2 * blk, blk), 1)
    cur_ok = (key >= blk) & (key - blk <= qry)
    prev_ok = (key < blk) & (key > qry)
    bias_s[0] = jnp.where(cur_ok, 0.0, -jnp.inf)
    bias_s[1] = jnp.where(cur_ok | prev_ok, 0.0, -jnp.inf)
    sinks = [jnp.full((1, blk), sinks_ref[kvh * g + c], F32) for c in range(g)]
    contract_last = (((1,), (1,)), ((), ()))
    contract_first = (((0,), (0,)), ((), ()))

    def body(n, carry):
        r0 = pl.multiple_of(n * blk, blk)
        rp = pl.multiple_of(jnp.maximum(n - 1, 0) * blk, blk)
        kw = jnp.concatenate([k_ref[pl.ds(rp, blk), :], k_ref[pl.ds(r0, blk), :]], axis=0)
        vw = jnp.concatenate([v_ref[pl.ds(rp, blk), :], v_ref[pl.ds(r0, blk), :]], axis=0)
        bias = bias_s[jnp.minimum(n, 1)]
        for c in range(g):
            qc = q_ref[pl.ds(r0, blk), c * HEAD_DIM:(c + 1) * HEAD_DIM]
            s = lax.dot_general(kw, qc, contract_last, preferred_element_type=F32) + bias
            m = jnp.maximum(jnp.max(s, axis=0, keepdims=True), sinks[c])
            e = jnp.exp(s - m)
            denom = jnp.sum(e, axis=0, keepdims=True) + jnp.exp(sinks[c] - m)
            p = (e * (1.0 / denom)).astype(BF16)
            o = lax.dot_general(p, vw, contract_first, preferred_element_type=F32)
            o_ref[pl.ds(r0, blk), c * HEAD_DIM:(c + 1) * HEAD_DIM] = o.astype(o_ref.dtype)
        return carry

    lax.fori_loop(0, seq // blk, body, 0, unroll=16)


def _attention(qkv, sinks, batch, seq):
    m = qkv.shape[0]
    qw = GQA_GROUP * HEAD_DIM
    k_blk0 = ATTN_WIDTH // HEAD_DIM
    v_blk0 = (ATTN_WIDTH + KV_WIDTH) // HEAD_DIM
    return pl.pallas_call(
        functools.partial(_attn_kernel, seq=seq),
        grid=(batch, N_KV_HEADS),
        in_specs=[pl.BlockSpec(memory_space=pltpu.SMEM),
                  pl.BlockSpec((seq, qw), lambda b, h: (b, h)),
                  pl.BlockSpec((seq, HEAD_DIM), lambda b, h: (b, k_blk0 + h)),
                  pl.BlockSpec((seq, HEAD_DIM), lambda b, h: (b, v_blk0 + h))],
        out_specs=pl.BlockSpec((seq, qw), lambda b, h: (b, h)),
        out_shape=jax.ShapeDtypeStruct((m, ATTN_WIDTH), BF16),
        scratch_shapes=[pltpu.VMEM((2, 2 * BLOCK, BLOCK), F32)],
        compiler_params=_params(2),
        name="swa_attention",
    )(sinks, qkv, qkv, qkv)


def _pool_kernel(p_ref, w_ref, sc_ref, o_ref, pad_s, y_s, *, seq, rows):
    grp = pl.program_id(1)
    gw = POOL_GROUP_WIDTH
    pad_s[0:POOL_HALO, :] = jnp.zeros((POOL_HALO, gw), F32)
    pad_s[POOL_HALO:, :] = p_ref[...]
    ext = rows + POOL_HALO
    lrow = lax.broadcasted_iota(jnp.int32, (rows, 128), 0)

    for gi, w in enumerate(POOL_WINDOWS):
        assert w & (w - 1) == 0 and w - 1 <= POOL_HALO

        @pl.when(grp == gi)
        def _(w=w):
            def body(r, carry):
                r0 = pl.multiple_of(r * rows, rows)
                cnt = jnp.minimum(lrow + (r0 + 1), w).astype(F32)
                for c in range(gw // 128):
                    x = pad_s[pl.ds(r0, ext), c * 128:(c + 1) * 128]
                    s = x
                    d = 1
                    while d < w:
                        s = s + pltpu.roll(s, d, 0)
                        d *= 2
                    y = s[POOL_HALO:] / cnt - x[POOL_HALO:]
                    y_s[pl.ds(r0, rows), c * 128:(c + 1) * 128] = y.astype(BF16)
                return carry

            lax.fori_loop(0, seq // rows, body, 0)

    out = jnp.dot(y_s[...], w_ref[0].astype(BF16), preferred_element_type=F32) * sc_ref[...]
    o_ref[...] = out.astype(o_ref.dtype)


def _pool(proj, pool_w, pool_scale, batch, seq, rows=128):
    m = proj.shape[0]
    gw = POOL_GROUP_WIDTH
    ng = len(POOL_WINDOWS)
    return pl.pallas_call(
        functools.partial(_pool_kernel, seq=seq, rows=rows),
        grid=(batch, ng),
        in_specs=[pl.BlockSpec((seq, gw), lambda b, g: (b, g)),
                  pl.BlockSpec((1, gw, gw), lambda b, g: (g, 0, 0)),
                  pl.BlockSpec((1, gw), lambda b, g: (0, g))],
        out_specs=pl.BlockSpec((seq, gw), lambda b, g: (b, g)),
        out_shape=jax.ShapeDtypeStruct((m, POOL_WIDTH), BF16),
        scratch_shapes=[pltpu.VMEM((seq + POOL_HALO, gw), F32), pltpu.VMEM((seq, gw), BF16)],
        compiler_params=_params(2),
        name="multiscale_pool",
    )(proj, pool_w, pool_scale)


def _rope_tables(seq):
    pos = np.arange(seq, dtype=np.float32)
    inv_freq = (np.float32(ROPE_THETA)
                ** (-np.arange(0, HEAD_DIM, 2, dtype=np.float32) / np.float32(HEAD_DIM)))
    ang = (pos[:, None] * inv_freq[None, :]).astype(np.float32)
    cos, sin = np.cos(ang).astype(np.float32), np.sin(ang).astype(np.float32)
    cos = np.concatenate([cos, cos], axis=-1)
    sin = np.concatenate([-sin, sin], axis=-1)
    scale = np.float32(HEAD_DIM ** -0.5)
    return jnp.asarray(np.stack([cos * scale, cos])), jnp.asarray(np.stack([sin * scale, sin]))


def kernel(x, ffn1_pre_g, ffn1_w_gate, ffn1_w_up, ffn1_w_down, ffn1_post_g, mix_pre_g, w_in, attn_sinks, pool_w, pool_scale, w_out, mix_post_g, ffn2_pre_g, ffn2_w_gate, ffn2_w_up, ffn2_w_down, ffn2_post_g):
    b, s, d = x.shape
    depth = ffn1_pre_g.shape[0]
    cos2, sin2 = _rope_tables(s)
    vec = lambda g: g.reshape(1, -1)
    xf = x.reshape(b * s, d)
    h = _prenorm(xf, vec(ffn1_pre_g[0]))
    for l in range(depth):
        side = [(w_in[l], SIDE_CAST_ROWS, ATTN_WIDTH + 2 * KV_WIDTH),
                (w_out[l], SIDE_CAST_ROWS, d)]
        y, (w_qkv, w_o) = _ffn(h, ffn1_w_gate[l], ffn1_w_up[l], ffn1_w_down[l], side)
        xf, h, qkv = _residual_qkv(xf, y, vec(ffn1_post_g[l]), vec(mix_pre_g[l]), w_qkv,
                                   cos2, sin2, s, FFN_RES_WEIGHT)
        a = _attention(qkv, attn_sinks[l], b, s)
        pm = _pool(_in_proj_pool(h, w_in[l]), pool_w[l], vec(pool_scale[l]), b, s)
        xf, h = _out_proj_residual(a, pm, xf, vec(mix_post_g[l]), vec(ffn2_pre_g[l]), w_o)

        y, _ = _ffn(h, ffn2_w_gate[l], ffn2_w_up[l], ffn2_w_down[l])
        g_next = vec(ffn1_pre_g[l + 1]) if l + 1 < depth else None
        xf, h = _residual(xf, y, vec(ffn2_post_g[l]), g_next, FFN_RES_WEIGHT)
    return xf.reshape(b, s, d)
```

```python
import functools

import jax
import jax.numpy as jnp
import numpy as np
from jax import lax
from jax.experimental import pallas as pl
from jax.experimental.pallas import tpu as pltpu

D_MODEL = 4096
HEAD_DIM = 128
N_HEADS = 16
N_KV_HEADS = 4
GQA_GROUP = N_HEADS // N_KV_HEADS
ATTN_WIDTH = N_HEADS * HEAD_DIM
KV_WIDTH = N_KV_HEADS * HEAD_DIM
BLOCK = 128
ROPE_THETA = 10000.0
POOL_WIDTH = D_MODEL - ATTN_WIDTH
POOL_WINDOWS = (2, 4, 8, 16)
POOL_GROUP_WIDTH = POOL_WIDTH // len(POOL_WINDOWS)
IN_PROJ_WIDTH = ATTN_WIDTH + 2 * KV_WIDTH + POOL_WIDTH
FFN_RES_WEIGHT = 0.5
RMS_EPS = 1e-6
POOL_HALO = 16

BF16 = jnp.bfloat16
F32 = jnp.float32
BF16_SUBLANES = 16
SIDE_CAST_ROWS = 32

VMEM_LIMIT_BYTES = 62 * 1024 * 1024


def _params(n_axes):
    return pltpu.CompilerParams(
        dimension_semantics=("arbitrary",) * n_axes,
        vmem_limit_bytes=VMEM_LIMIT_BYTES,
    )


def _rms(x, g):
    ms = jnp.mean(x * x, axis=-1, keepdims=True)
    return x * lax.rsqrt(ms + RMS_EPS) * g


def _norm_kernel(x_ref, g_ref, h_ref):
    h_ref[...] = _rms(x_ref[...], g_ref[...]).astype(h_ref.dtype)


def _prenorm(x, g, tm=512):
    m, d = x.shape
    return pl.pallas_call(
        _norm_kernel,
        grid=(m // tm,),
        in_specs=[pl.BlockSpec((tm, d), lambda i: (i, 0)),
                  pl.BlockSpec((1, d), lambda i: (0, 0))],
        out_specs=pl.BlockSpec((tm, d), lambda i: (i, 0)),
        out_shape=jax.ShapeDtypeStruct((m, d), BF16),
        compiler_params=_params(1),
        name="prenorm",
    )(x, g)


def _residual_kernel(x_ref, y_ref, gpost_ref, *rest, weight, with_next):
    if with_next:
        gnext_ref, xo_ref, h_ref = rest
    else:
        (xo_ref,) = rest
    r = _rms(y_ref[...], gpost_ref[...])
    if weight != 1.0:
        r = weight * r
    xn = x_ref[...] + r
    xo_ref[...] = xn
    if with_next:
        h_ref[...] = _rms(xn, gnext_ref[...]).astype(h_ref.dtype)


def _residual(x, y, g_post, g_next, weight, tm=256):
    m, d = x.shape
    with_next = g_next is not None
    row = pl.BlockSpec((tm, d), lambda i: (i, 0))
    vec = pl.BlockSpec((1, d), lambda i: (0, 0))
    in_specs = [row, row, vec] + ([vec] if with_next else [])
    args = (x, y, g_post) + ((g_next,) if with_next else ())
    out_specs = [row] + ([row] if with_next else [])
    out_shape = [jax.ShapeDtypeStruct((m, d), F32)] + (
        [jax.ShapeDtypeStruct((m, d), BF16)] if with_next else [])
    outs = pl.pallas_call(
        functools.partial(_residual_kernel, weight=weight, with_next=with_next),
        grid=(m // tm,),
        in_specs=in_specs,
        out_specs=out_specs,
        out_shape=out_shape,
        compiler_params=_params(1),
        name="residual_norm",
    )(*args)
    return (outs[0], outs[1]) if with_next else (outs[0], None)


def _staged_kernel(*refs, n_x, n_w, side_steps, nj, slab, epilogue):
    n_side = len(side_steps)
    n_in = n_x + n_w
    x_refs = refs[:n_x]
    w_refs = refs[n_x:n_in]
    side_in = refs[n_in:n_in + n_side]
    n_in += n_side
    o_ref = refs[n_in]
    side_out = refs[n_in + 1:n_in + 1 + n_side]
    w_s = refs[n_in + 1 + n_side:]
    j = pl.program_id(0)
    i = pl.program_id(1)

    for src, dst, steps in zip(side_in, side_out, side_steps):
        @pl.when(j * pl.num_programs(1) + i < steps)
        def _(src=src, dst=dst):
            dst[...] = src[...].astype(dst.dtype)

    @pl.when(j < nj)
    def _():
        rows = pl.ds(pl.multiple_of(i * slab, slab), slab)
        for w_ref, s in zip(w_refs, w_s):
            s[j % 2, rows, :] = w_ref[...].astype(BF16)

    @pl.when(j > 0)
    def _():
        slot = (j + 1) % 2
        accs = []
        for s in w_s:
            acc, k0 = None, 0
            for x_ref in x_refs:
                kx = x_ref.shape[1]
                part = jnp.dot(x_ref[...], s[slot, k0:k0 + kx, :], preferred_element_type=F32)
                acc = part if acc is None else acc + part
                k0 += kx
            accs.append(acc)
        epilogue(accs, o_ref)


def _staged_matmul(xs, ws, *, n, out_dtype, epilogue, name, tm, tn, col0=0, side_casts=()):
    m = xs[0].shape[0]
    k = sum(x.shape[1] for x in xs)
    assert all(w.shape[0] == k for w in ws) and m % tm == 0 and col0 % tn == 0 and n % tn == 0
    ni, nj = m // tm, n // tn
    assert k % (ni * BF16_SUBLANES) == 0
    slab = k // ni
    cb = col0 // tn
    side_in, side_specs, side_shapes, side_steps = [], [], [], []
    for arr, rows, cols in side_casts:
        steps = arr.shape[0] // rows
        assert arr.shape[0] % rows == 0 and rows % BF16_SUBLANES == 0 and steps <= (nj + 1) * ni
        side_in.append(arr)
        side_specs.append(pl.BlockSpec(
            (rows, cols), lambda j, i, steps=steps: (jnp.minimum(j * ni + i, steps - 1), 0)))
        side_shapes.append(jax.ShapeDtypeStruct((arr.shape[0], cols), BF16))
        side_steps.append(steps)
    row_tile = lambda j, i: jnp.where(j == 0, 0, i)
    x_specs = [pl.BlockSpec((tm, x.shape[1]), lambda j, i: (row_tile(j, i), 0)) for x in xs]
    w_spec = pl.BlockSpec((slab, tn), lambda j, i: (jnp.where(j < nj, i, ni - 1),
                                                    cb + jnp.minimum(j, nj - 1)))
    o_spec = pl.BlockSpec((tm, tn), lambda j, i: (row_tile(j, i), jnp.maximum(j - 1, 0)))
    kern = functools.partial(_staged_kernel, n_x=len(xs), n_w=len(ws),
                             side_steps=tuple(side_steps), nj=nj, slab=slab, epilogue=epilogue)
    outs = pl.pallas_call(
        kern,
        grid=(nj + 1, ni),
        in_specs=x_specs + [w_spec] * len(ws) + side_specs,
        out_specs=[o_spec] + side_specs,
        out_shape=[jax.ShapeDtypeStruct((m, n), out_dtype)] + side_shapes,
        scratch_shapes=[pltpu.VMEM((2, k, tn), BF16) for _ in ws],
        compiler_params=_params(2),
        name=name,
    )(*xs, *ws, *side_in)
    return outs


def _store_epilogue(accs, o_ref):
    o_ref[...] = accs[0].astype(o_ref.dtype)


def _swiglu_epilogue(accs, o_ref):
    g, u = accs
    o_ref[...] = (jax.nn.silu(g) * u).astype(o_ref.dtype)


def _rope(t, cos, sin_signed):
    return t * cos + pltpu.roll(t, HEAD_DIM // 2, 1) * sin_signed


def _ffn(h, w_gate, w_up, w_down, side_casts=()):
    d_ff = w_gate.shape[1]
    tm, tn, tn_tail = 1024, 512, 256
    n_main = d_ff // tn * tn
    a, *copies = _staged_matmul([h], [w_gate, w_up], n=n_main, out_dtype=BF16,
                                epilogue=_swiglu_epilogue, name="ffn_gate_up", tm=tm, tn=tn,
                                side_casts=side_casts)
    parts = [a]
    if n_main < d_ff:
        parts += _staged_matmul([h], [w_gate, w_up], n=d_ff - n_main, col0=n_main,
                                out_dtype=BF16, epilogue=_swiglu_epilogue,
                                name="ffn_gate_up_tail", tm=2 * tm, tn=tn_tail)
    y, = _staged_matmul(parts, [w_down], n=w_down.shape[1], out_dtype=F32,
                        epilogue=_store_epilogue, name="ffn_down", tm=512, tn=512)
    return y, copies


def _in_proj_pool(h, w, tm=1024, tn=512):
    return _staged_matmul([h], [w], n=POOL_WIDTH, out_dtype=F32, epilogue=_store_epilogue,
                          name="mix_in_proj_pool", tm=tm, tn=tn, col0=ATTN_WIDTH + 2 * KV_WIDTH)[0]


def _resident(shape):
    return pl.BlockSpec(shape, lambda t: (0,) * len(shape), pipeline_mode=pl.Buffered(1))


def _residual_qkv_kernel(x_ref, y_ref, gpost_ref, gnext_ref, w_ref, cos_ref, sin_ref,
                         xo_ref, h_ref, qkv_ref, *, weight):
    xn = x_ref[...] + weight * _rms(y_ref[...], gpost_ref[...])
    xo_ref[...] = xn
    h = _rms(xn, gnext_ref[...]).astype(BF16)
    h_ref[...] = h
    acc = jnp.dot(h, w_ref[...], preferred_element_type=F32)
    for c in range(acc.shape[1] // HEAD_DIM):
        cols = slice(c * HEAD_DIM, (c + 1) * HEAD_DIM)
        if c * HEAD_DIM < ATTN_WIDTH + KV_WIDTH:
            kind = 0 if c * HEAD_DIM < ATTN_WIDTH else 1
            val = _rope(acc[:, cols], cos_ref[kind], sin_ref[kind])
        else:
            val = acc[:, cols]
        qkv_ref[:, cols] = val.astype(qkv_ref.dtype)


def _residual_qkv(x, y, g_post, g_next, w_qkv, cos2, sin2, seq, weight, tm=256):
    m, d = x.shape
    n = ATTN_WIDTH + 2 * KV_WIDTH
    assert w_qkv.shape == (d, n) and w_qkv.dtype == BF16 and m % tm == 0 and seq % tm == 0
    row = lambda width: pl.BlockSpec((tm, width), lambda t: (t, 0))
    vec = pl.BlockSpec((1, d), lambda t: (0, 0))
    table = pl.BlockSpec((2, tm, HEAD_DIM), lambda t: (0, t % (seq // tm), 0))
    return pl.pallas_call(
        functools.partial(_residual_qkv_kernel, weight=weight),
        grid=(m // tm,),
        in_specs=[row(d), row(d), vec, vec, _resident((d, n)), table, table],
        out_specs=[row(d), row(d), row(n)],
        out_shape=[jax.ShapeDtypeStruct((m, d), F32), jax.ShapeDtypeStruct((m, d), BF16),
                   jax.ShapeDtypeStruct((m, n), BF16)],
        compiler_params=_params(1),
        name="residual_qkv",
    )(x, y, g_post, g_next, w_qkv, cos2, sin2)


def _out_proj_residual_kernel(a_ref, p_ref, x_ref, gpost_ref, gnext_ref, w_ref, xo_ref, h_ref):
    ka = a_ref.shape[1]
    y = (jnp.dot(a_ref[...], w_ref[0:ka, :], preferred_element_type=F32)
         + jnp.dot(p_ref[...], w_ref[ka:, :], preferred_element_type=F32))
    xn = x_ref[...] + _rms(y, gpost_ref[...])
    xo_ref[...] = xn
    h_ref[...] = _rms(xn, gnext_ref[...]).astype(h_ref.dtype)


def _out_proj_residual(a, pm, x, g_post, g_next, w_bf16, tm=256):
    m, d = x.shape
    k = a.shape[1] + pm.shape[1]
    assert w_bf16.shape == (k, d) and w_bf16.dtype == BF16 and m % tm == 0
    row = lambda width: pl.BlockSpec((tm, width), lambda t: (t, 0))
    vec = pl.BlockSpec((1, d), lambda t: (0, 0))
    return pl.pallas_call(
        _out_proj_residual_kernel,
        grid=(m // tm,),
        in_specs=[row(a.shape[1]), row(pm.shape[1]), row(d), vec, vec, _resident((k, d))],
        out_specs=[row(d), row(d)],
        out_shape=[jax.ShapeDtypeStruct((m, d), F32), jax.ShapeDtypeStruct((m, d), BF16)],
        compiler_params=_params(1),
        name="out_proj_residual",
    )(a, pm, x, g_post, g_next, w_bf16)


def _attn_kernel(sinks_ref, q_ref, k_ref, v_ref, o_ref, bias_s, *, seq):
    kvh = pl.program_id(1)
    g, blk = GQA_GROUP, BLOCK
    key = lax.broadcasted_iota(jnp.int32, (2 * blk, blk), 0)
    qry = lax.broadcasted_iota(jnp.int32, (2 * blk, blk), 1)
    cur_ok = (key >= blk) & (key - blk <= qry)
    prev_ok = (key < blk) & (key > qry)
    bias_s[0] = jnp.where(cur_ok, 0.0, -jnp.inf)
    bias_s[1] = jnp.where(cur_ok | prev_ok, 0.0, -jnp.inf)
    sinks = [jnp.full((1, blk), sinks_ref[kvh * g + c], F32) for c in range(g)]
    contract_last = (((1,), (1,)), ((), ()))
    contract_first = (((0,), (0,)), ((), ()))

    def body(n, carry):
        r0 = pl.multiple_of(n * blk, blk)
        rp = pl.multiple_of(jnp.maximum(n - 1, 0) * blk, blk)
        kw = jnp.concatenate([k_ref[pl.ds(rp, blk), :], k_ref[pl.ds(r0, blk), :]], axis=0)
        vw = jnp.concatenate([v_ref[pl.ds(rp, blk), :], v_ref[pl.ds(r0, blk), :]], axis=0)
        bias = bias_s[jnp.minimum(n, 1)]
        for c in range(g):
            qc = q_ref[pl.ds(r0, blk), c * HEAD_DIM:(c + 1) * HEAD_DIM]
            s = lax.dot_general(kw, qc, contract_last, preferred_element_type=F32) + bias
            m = jnp.maximum(jnp.max(s, axis=0, keepdims=True), sinks[c])
            e = jnp.exp(s - m)
            denom = jnp.sum(e, axis=0, keepdims=True) + jnp.exp(sinks[c] - m)
            p = (e * (1.0 / denom)).astype(BF16)
            o = lax.dot_general(p, vw, contract_first, preferred_element_type=F32)
            o_ref[pl.ds(r0, blk), c * HEAD_DIM:(c + 1) * HEAD_DIM] = o.astype(o_ref.dtype)
        return carry

    lax.fori_loop(0, seq // blk, body, 0, unroll=16)


def _attention(qkv, sinks, batch, seq):
    m = qkv.shape[0]
    qw = GQA_GROUP * HEAD_DIM
    k_blk0 = ATTN_WIDTH // HEAD_DIM
    v_blk0 = (ATTN_WIDTH + KV_WIDTH) // HEAD_DIM
    return pl.pallas_call(
        functools.partial(_attn_kernel, seq=seq),
        grid=(batch, N_KV_HEADS),
        in_specs=[pl.BlockSpec(memory_space=pltpu.SMEM),
                  pl.BlockSpec((seq, qw), lambda b, h: (b, h)),
                  pl.BlockSpec((seq, HEAD_DIM), lambda b, h: (b, k_blk0 + h)),
                  pl.BlockSpec((seq, HEAD_DIM), lambda b, h: (b, v_blk0 + h))],
        out_specs=pl.BlockSpec((seq, qw), lambda b, h: (b, h)),
        out_shape=jax.ShapeDtypeStruct((m, ATTN_WIDTH), BF16),
        scratch_shapes=[pltpu.VMEM((2, 2 * BLOCK, BLOCK), F32)],
        compiler_params=_params(2),
        name="swa_attention",
    )(sinks, qkv, qkv, qkv)


def _pool_kernel(p_ref, w_ref, sc_ref, o_ref, pad_s, y_s, *, seq, rows):
    grp = pl.program_id(1)
    gw = POOL_GROUP_WIDTH
    pad_s[0:POOL_HALO, :] = jnp.zeros((POOL_HALO, gw), F32)
    pad_s[POOL_HALO:, :] = p_ref[...]
    ext = rows + POOL_HALO
    lrow = lax.broadcasted_iota(jnp.int32, (rows, 128), 0)

    for gi, w in enumerate(POOL_WINDOWS):
        assert w & (w - 1) == 0 and w - 1 <= POOL_HALO

        @pl.when(grp == gi)
        def _(w=w):
            def body(r, carry):
                r0 = pl.multiple_of(r * rows, rows)
                cnt = jnp.minimum(lrow + (r0 + 1), w).astype(F32)
                for c in range(gw // 128):
                    x = pad_s[pl.ds(r0, ext), c * 128:(c + 1) * 128]
                    s = x
                    d = 1
                    while d < w:
                        s = s + pltpu.roll(s, d, 0)
                        d *= 2
                    y = s[POOL_HALO:] / cnt - x[POOL_HALO:]
                    y_s[pl.ds(r0, rows), c * 128:(c + 1) * 128] = y.astype(BF16)
                return carry

            lax.fori_loop(0, seq // rows, body, 0)

    out = jnp.dot(y_s[...], w_ref[0].astype(BF16), preferred_element_type=F32) * sc_ref[...]
    o_ref[...] = out.astype(o_ref.dtype)


def _pool(proj, pool_w, pool_scale, batch, seq, rows=128):
    m = proj.shape[0]
    gw = POOL_GROUP_WIDTH
    ng = len(POOL_WINDOWS)
    return pl.pallas_call(
        functools.partial(_pool_kernel, seq=seq, rows=rows),
        grid=(batch, ng),
        in_specs=[pl.BlockSpec((seq, gw), lambda b, g: (b, g)),
                  pl.BlockSpec((1, gw, gw), lambda b, g: (g, 0, 0)),
                  pl.BlockSpec((1, gw), lambda b, g: (0, g))],
        out_specs=pl.BlockSpec((seq, gw), lambda b, g: (b, g)),
        out_shape=jax.ShapeDtypeStruct((m, POOL_WIDTH), BF16),
        scratch_shapes=[pltpu.VMEM((seq + POOL_HALO, gw), F32), pltpu.VMEM((seq, gw), BF16)],
        compiler_params=_params(2),
        name="multiscale_pool",
    )(proj, pool_w, pool_scale)


def _rope_tables(seq):
    pos = np.arange(seq, dtype=np.float32)
    inv_freq = (np.float32(ROPE_THETA)
                ** (-np.arange(0, HEAD_DIM, 2, dtype=np.float32) / np.float32(HEAD_DIM)))
    ang = (pos[:, None] * inv_freq[None, :]).astype(np.float32)
    cos, sin = np.cos(ang).astype(np.float32), np.sin(ang).astype(np.float32)
    cos = np.concatenate([cos, cos], axis=-1)
    sin = np.concatenate([-sin, sin], axis=-1)
    scale = np.float32(HEAD_DIM ** -0.5)
    return jnp.asarray(np.stack([cos * scale, cos])), jnp.asarray(np.stack([sin * scale, sin]))


def kernel(x, ffn1_pre_g, ffn1_w_gate, ffn1_w_up, ffn1_w_down, ffn1_post_g, mix_pre_g, w_in, attn_sinks, pool_w, pool_scale, w_out, mix_post_g, ffn2_pre_g, ffn2_w_gate, ffn2_w_up, ffn2_w_down, ffn2_post_g):
    b, s, d = x.shape
    depth = ffn1_pre_g.shape[0]
    cos2, sin2 = _rope_tables(s)
    vec = lambda g: g.reshape(1, -1)
    xf = x.reshape(b * s, d)
    h = _prenorm(xf, vec(ffn1_pre_g[0]))
    for l in range(depth):
        side = [(w_in[l], SIDE_CAST_ROWS, ATTN_WIDTH + 2 * KV_WIDTH),
                (w_out[l], SIDE_CAST_ROWS, d)]
        y, (w_qkv, w_o) = _ffn(h, ffn1_w_gate[l], ffn1_w_up[l], ffn1_w_down[l], side)
        xf, h, qkv = _residual_qkv(xf, y, vec(ffn1_post_g[l]), vec(mix_pre_g[l]), w_qkv,
                                   cos2, sin2, s, FFN_RES_WEIGHT)
        a = _attention(qkv, attn_sinks[l], b, s)
        pm = _pool(_in_proj_pool(h, w_in[l]), pool_w[l], vec(pool_scale[l]), b, s)
        xf, h = _out_proj_residual(a, pm, xf, vec(mix_post_g[l]), vec(ffn2_pre_g[l]), w_o)

        y, _ = _ffn(h, ffn2_w_gate[l], ffn2_w_up[l], ffn2_w_down[l])
        g_next = vec(ffn1_pre_g[l + 1]) if l + 1 < depth else None
        xf, h = _residual(xf, y, vec(ffn2_post_g[l]), g_next, FFN_RES_WEIGHT)
    return xf.reshape(b, s, d)
```

```python
import functools

import jax
import jax.numpy as jnp
import numpy as np
from jax import lax
from jax.experimental import pallas as pl
from jax.experimental.pallas import tpu as pltpu

D_MODEL = 4096
HEAD_DIM = 128
N_HEADS = 16
N_KV_HEADS = 4
GQA_GROUP = N_HEADS // N_KV_HEADS
ATTN_WIDTH = N_HEADS * HEAD_DIM
KV_WIDTH = N_KV_HEADS * HEAD_DIM
BLOCK = 128
ROPE_THETA = 10000.0
POOL_WIDTH = D_MODEL - ATTN_WIDTH
POOL_WINDOWS = (2, 4, 8, 16)
POOL_GROUP_WIDTH = POOL_WIDTH // len(POOL_WINDOWS)
IN_PROJ_WIDTH = ATTN_WIDTH + 2 * KV_WIDTH + POOL_WIDTH
FFN_RES_WEIGHT = 0.5
RMS_EPS = 1e-6
LOG2_E = 1.4426950408889634
POOL_HALO = 16

BF16 = jnp.bfloat16
F32 = jnp.float32
BF16_SUBLANES = 16
SIDE_CAST_ROWS = 32

VMEM_LIMIT_BYTES = 62 * 1024 * 1024


def _params(n_axes):
    return pltpu.CompilerParams(
        dimension_semantics=("arbitrary",) * n_axes,
        vmem_limit_bytes=VMEM_LIMIT_BYTES,
    )


def _rms(x, g):
    ms = jnp.mean(x * x, axis=-1, keepdims=True)
    return x * lax.rsqrt(ms + RMS_EPS) * g


def _norm_kernel(x_ref, g_ref, h_ref):
    h_ref[...] = _rms(x_ref[...], g_ref[...]).astype(h_ref.dtype)


def _prenorm(x, g, tm=512):
    m, d = x.shape
    return pl.pallas_call(
        _norm_kernel,
        grid=(m // tm,),
        in_specs=[pl.BlockSpec((tm, d), lambda i: (i, 0)),
                  pl.BlockSpec((1, d), lambda i: (0, 0))],
        out_specs=pl.BlockSpec((tm, d), lambda i: (i, 0)),
        out_shape=jax.ShapeDtypeStruct((m, d), BF16),
        compiler_params=_params(1),
        name="prenorm",
    )(x, g)


def _residual_kernel(x_ref, y_ref, gpost_ref, *rest, weight, with_next):
    if with_next:
        gnext_ref, xo_ref, h_ref = rest
    else:
        (xo_ref,) = rest
    r = _rms(y_ref[...], gpost_ref[...])
    if weight != 1.0:
        r = weight * r
    xn = x_ref[...] + r
    xo_ref[...] = xn
    if with_next:
        h_ref[...] = _rms(xn, gnext_ref[...]).astype(h_ref.dtype)


def _residual(x, y, g_post, g_next, weight, tm=256):
    m, d = x.shape
    with_next = g_next is not None
    row = pl.BlockSpec((tm, d), lambda i: (i, 0))
    vec = pl.BlockSpec((1, d), lambda i: (0, 0))
    in_specs = [row, row, vec] + ([vec] if with_next else [])
    args = (x, y, g_post) + ((g_next,) if with_next else ())
    out_specs = [row] + ([row] if with_next else [])
    out_shape = [jax.ShapeDtypeStruct((m, d), F32)] + (
        [jax.ShapeDtypeStruct((m, d), BF16)] if with_next else [])
    outs = pl.pallas_call(
        functools.partial(_residual_kernel, weight=weight, with_next=with_next),
        grid=(m // tm,),
        in_specs=in_specs,
        out_specs=out_specs,
        out_shape=out_shape,
        compiler_params=_params(1),
        name="residual_norm",
    )(*args)
    return (outs[0], outs[1]) if with_next else (outs[0], None)


def _staged_kernel(*refs, n_x, n_w, side_steps, nj, slab, epilogue):
    n_side = len(side_steps)
    n_in = n_x + n_w
    x_refs = refs[:n_x]
    w_refs = refs[n_x:n_in]
    side_in = refs[n_in:n_in + n_side]
    n_in += n_side
    o_ref = refs[n_in]
    side_out = refs[n_in + 1:n_in + 1 + n_side]
    w_s = refs[n_in + 1 + n_side:]
    j = pl.program_id(0)
    i = pl.program_id(1)

    for src, dst, steps in zip(side_in, side_out, side_steps):
        @pl.when(j * pl.num_programs(1) + i < steps)
        def _(src=src, dst=dst):
            dst[...] = src[...].astype(dst.dtype)

    @pl.when(j < nj)
    def _():
        rows = pl.ds(pl.multiple_of(i * slab, slab), slab)
        for w_ref, s in zip(w_refs, w_s):
            s[j % 2, rows, :] = w_ref[...].astype(BF16)

    @pl.when(j > 0)
    def _():
        slot = (j + 1) % 2
        accs = []
        for s in w_s:
            acc, k0 = None, 0
            for x_ref in x_refs:
                kx = x_ref.shape[1]
                part = jnp.dot(x_ref[...], s[slot, k0:k0 + kx, :], preferred_element_type=F32)
                acc = part if acc is None else acc + part
                k0 += kx
            accs.append(acc)
        epilogue(accs, o_ref)


def _staged_matmul(xs, ws, *, n, out_dtype, epilogue, name, tm, tn, col0=0, side_casts=()):
    m = xs[0].shape[0]
    k = sum(x.shape[1] for x in xs)
    assert all(w.shape[0] == k for w in ws) and m % tm == 0 and col0 % tn == 0 and n % tn == 0
    ni, nj = m // tm, n // tn
    assert k % (ni * BF16_SUBLANES) == 0
    slab = k // ni
    cb = col0 // tn
    side_in, side_specs, side_shapes, side_steps = [], [], [], []
    for arr, rows, cols in side_casts:
        steps = arr.shape[0] // rows
        assert arr.shape[0] % rows == 0 and rows % BF16_SUBLANES == 0 and steps <= (nj + 1) * ni
        side_in.append(arr)
        side_specs.append(pl.BlockSpec(
            (rows, cols), lambda j, i, steps=steps: (jnp.minimum(j * ni + i, steps - 1), 0)))
        side_shapes.append(jax.ShapeDtypeStruct((arr.shape[0], cols), BF16))
        side_steps.append(steps)
    row_tile = lambda j, i: jnp.where(j == 0, 0, i)
    x_specs = [pl.BlockSpec((tm, x.shape[1]), lambda j, i: (row_tile(j, i), 0)) for x in xs]
    w_spec = pl.BlockSpec((slab, tn), lambda j, i: (jnp.where(j < nj, i, ni - 1),
                                                    cb + jnp.minimum(j, nj - 1)))
    o_spec = pl.BlockSpec((tm, tn), lambda j, i: (row_tile(j, i), jnp.maximum(j - 1, 0)))
    kern = functools.partial(_staged_kernel, n_x=len(xs), n_w=len(ws),
                             side_steps=tuple(side_steps), nj=nj, slab=slab, epilogue=epilogue)
    outs = pl.pallas_call(
        kern,
        grid=(nj + 1, ni),
        in_specs=x_specs + [w_spec] * len(ws) + side_specs,
        out_specs=[o_spec] + side_specs,
        out_shape=[jax.ShapeDtypeStruct((m, n), out_dtype)] + side_shapes,
        scratch_shapes=[pltpu.VMEM((2, k, tn), BF16) for _ in ws],
        compiler_params=_params(2),
        name=name,
    )(*xs, *ws, *side_in)
    return outs


def _store_epilogue(accs, o_ref):
    o_ref[...] = accs[0].astype(o_ref.dtype)


def _swiglu_epilogue(accs, o_ref):
    g, u = accs
    o_ref[...] = (jax.nn.silu(g) * u).astype(o_ref.dtype)


def _rope(t, cos, sin_signed):
    return t * cos + pltpu.roll(t, HEAD_DIM // 2, 1) * sin_signed


def _ffn(h, w_gate, w_up, w_down, side_casts=()):
    d_ff = w_gate.shape[1]
    tm, tn, tn_tail = 1024, 512, 256
    n_main = d_ff // tn * tn
    a, *copies = _staged_matmul([h], [w_gate, w_up], n=n_main, out_dtype=BF16,
                                epilogue=_swiglu_epilogue, name="ffn_gate_up", tm=tm, tn=tn,
                                side_casts=side_casts)
    parts = [a]
    if n_main < d_ff:
        parts += _staged_matmul([h], [w_gate, w_up], n=d_ff - n_main, col0=n_main,
                                out_dtype=BF16, epilogue=_swiglu_epilogue,
                                name="ffn_gate_up_tail", tm=2 * tm, tn=tn_tail)
    y, = _staged_matmul(parts, [w_down], n=w_down.shape[1], out_dtype=F32,
                        epilogue=_store_epilogue, name="ffn_down", tm=512, tn=512)
    return y, copies


def _in_proj_pool(h, w, tm=1024, tn=1024):
    return _staged_matmul([h], [w], n=POOL_WIDTH, out_dtype=F32, epilogue=_store_epilogue,
                          name="mix_in_proj_pool", tm=tm, tn=tn, col0=ATTN_WIDTH + 2 * KV_WIDTH)[0]


def _resident(shape):
    return pl.BlockSpec(shape, lambda t: (0,) * len(shape), pipeline_mode=pl.Buffered(1))


def _residual_qkv_kernel(x_ref, y_ref, gpost_ref, gnext_ref, w_ref, cos_ref, sin_ref,
                         xo_ref, h_ref, qkv_ref, *, weight):
    xn = x_ref[...] + weight * _rms(y_ref[...], gpost_ref[...])
    xo_ref[...] = xn
    h = _rms(xn, gnext_ref[...]).astype(BF16)
    h_ref[...] = h
    acc = jnp.dot(h, w_ref[...], preferred_element_type=F32)
    for c in range(acc.shape[1] // HEAD_DIM):
        cols = slice(c * HEAD_DIM, (c + 1) * HEAD_DIM)
        if c * HEAD_DIM < ATTN_WIDTH + KV_WIDTH:
            kind = 0 if c * HEAD_DIM < ATTN_WIDTH else 1
            val = _rope(acc[:, cols], cos_ref[kind], sin_ref[kind])
        else:
            val = acc[:, cols]
        qkv_ref[:, cols] = val.astype(qkv_ref.dtype)


def _residual_qkv(x, y, g_post, g_next, w_qkv, cos2, sin2, seq, weight, tm=256):
    m, d = x.shape
    n = ATTN_WIDTH + 2 * KV_WIDTH
    assert w_qkv.shape == (d, n) and w_qkv.dtype == BF16 and m % tm == 0 and seq % tm == 0
    row = lambda width: pl.BlockSpec((tm, width), lambda t: (t, 0))
    vec = pl.BlockSpec((1, d), lambda t: (0, 0))
    table = pl.BlockSpec((2, tm, HEAD_DIM), lambda t: (0, t % (seq // tm), 0))
    return pl.pallas_call(
        functools.partial(_residual_qkv_kernel, weight=weight),
        grid=(m // tm,),
        in_specs=[row(d), row(d), vec, vec, _resident((d, n)), table, table],
        out_specs=[row(d), row(d), row(n)],
        out_shape=[jax.ShapeDtypeStruct((m, d), F32), jax.ShapeDtypeStruct((m, d), BF16),
                   jax.ShapeDtypeStruct((m, n), BF16)],
        compiler_params=_params(1),
        name="residual_qkv",
    )(x, y, g_post, g_next, w_qkv, cos2, sin2)


def _out_proj_residual_kernel(a_ref, p_ref, x_ref, gpost_ref, gnext_ref, w_ref, xo_ref, h_ref):
    ka = a_ref.shape[1]
    y = (jnp.dot(a_ref[...], w_ref[0:ka, :], preferred_element_type=F32)
         + jnp.dot(p_ref[...], w_ref[ka:, :], preferred_element_type=F32))
    xn = x_ref[...] + _rms(y, gpost_ref[...])
    xo_ref[...] = xn
    h_ref[...] = _rms(xn, gnext_ref[...]).astype(h_ref.dtype)


def _out_proj_residual(a, pm, x, g_post, g_next, w_bf16, tm=256):
    m, d = x.shape
    k = a.shape[1] + pm.shape[1]
    assert w_bf16.shape == (k, d) and w_bf16.dtype == BF16 and m % tm == 0
    row = lambda width: pl.BlockSpec((tm, width), lambda t: (t, 0))
    vec = pl.BlockSpec((1, d), lambda t: (0, 0))
    return pl.pallas_call(
        _out_proj_residual_kernel,
        grid=(m // tm,),
        in_specs=[row(a.shape[1]), row(pm.shape[1]), row(d), vec, vec, _resident((k, d))],
        out_specs=[row(d), row(d)],
        out_shape=[jax.ShapeDtypeStruct((m, d), F32), jax.ShapeDtypeStruct((m, d), BF16)],
        compiler_params=_params(1),
        name="out_proj_residual",
    )(a, pm, x, g_post, g_next, w_bf16)


def _attn_kernel(sinks_ref, q_ref, k_ref, v_ref, o_ref, bias_s, *, seq):
    kvh = pl.program_id(1)
    g, blk = GQA_GROUP, BLOCK
    key = lax.broadcasted_iota(jnp.int32, (2 * blk, blk), 0)
    qry = lax.broadcasted_iota(jnp.int32, (2 * blk, blk), 1)
    cur_ok = (key >= blk) & (key - blk <= qry)
    prev_ok = (key < blk) & (key > qry)
    bias_s[0] = jnp.where(cur_ok, 0.0, -jnp.inf)
    bias_s[1] = jnp.where(cur_ok | prev_ok, 0.0, -jnp.inf)
    sinks = [jnp.full((1, blk), sinks_ref[kvh * g + c] * LOG2_E, F32) for c in range(g)]
    contract_last = (((1,), (1,)), ((), ()))
    contract_first = (((0,), (0,)), ((), ()))

    def body(n, carry):
        r0 = pl.multiple_of(n * blk, blk)
        rp = pl.multiple_of(jnp.maximum(n - 1, 0) * blk, blk)
        kw = jnp.concatenate([k_ref[pl.ds(rp, blk), :], k_ref[pl.ds(r0, blk), :]], axis=0)
        vw = jnp.concatenate([v_ref[pl.ds(rp, blk), :], v_ref[pl.ds(r0, blk), :]], axis=0)
        bias = bias_s[jnp.minimum(n, 1)]
        for c in range(g):
            qc = q_ref[pl.ds(r0, blk), c * HEAD_DIM:(c + 1) * HEAD_DIM]
            s = lax.dot_general(kw, qc, contract_last, preferred_element_type=F32) + bias
            m = jnp.maximum(jnp.max(s, axis=0, keepdims=True), sinks[c])
            e = jnp.exp2(s - m)
            denom = jnp.sum(e, axis=0, keepdims=True) + jnp.exp2(sinks[c] - m)
            p = (e * (1.0 / denom)).astype(BF16)
            o = lax.dot_general(p, vw, contract_first, preferred_element_type=F32)
            o_ref[pl.ds(r0, blk), c * HEAD_DIM:(c + 1) * HEAD_DIM] = o.astype(o_ref.dtype)
        return carry

    lax.fori_loop(0, seq // blk, body, 0, unroll=16)


def _attention(qkv, sinks, batch, seq):
    m = qkv.shape[0]
    qw = GQA_GROUP * HEAD_DIM
    k_blk0 = ATTN_WIDTH // HEAD_DIM
    v_blk0 = (ATTN_WIDTH + KV_WIDTH) // HEAD_DIM
    return pl.pallas_call(
        functools.partial(_attn_kernel, seq=seq),
        grid=(batch, N_KV_HEADS),
        in_specs=[pl.BlockSpec(memory_space=pltpu.SMEM),
                  pl.BlockSpec((seq, qw), lambda b, h: (b, h)),
                  pl.BlockSpec((seq, HEAD_DIM), lambda b, h: (b, k_blk0 + h)),
                  pl.BlockSpec((seq, HEAD_DIM), lambda b, h: (b, v_blk0 + h))],
        out_specs=pl.BlockSpec((seq, qw), lambda b, h: (b, h)),
        out_shape=jax.ShapeDtypeStruct((m, ATTN_WIDTH), BF16),
        scratch_shapes=[pltpu.VMEM((2, 2 * BLOCK, BLOCK), F32)],
        compiler_params=_params(2),
        name="swa_attention",
    )(sinks, qkv, qkv, qkv)


def _pool_kernel(p_ref, w_ref, sc_ref, o_ref, pad_s, y_s, *, seq, rows):
    grp = pl.program_id(1)
    gw = POOL_GROUP_WIDTH
    pad_s[0:POOL_HALO, :] = jnp.zeros((POOL_HALO, gw), F32)
    pad_s[POOL_HALO:, :] = p_ref[...]
    ext = rows + POOL_HALO
    lrow = lax.broadcasted_iota(jnp.int32, (rows, 128), 0)

    for gi, w in enumerate(POOL_WINDOWS):
        assert w & (w - 1) == 0 and w - 1 <= POOL_HALO

        @pl.when(grp == gi)
        def _(w=w):
            def body(r, carry):
                r0 = pl.multiple_of(r * rows, rows)
                cnt = jnp.minimum(lrow + (r0 + 1), w).astype(F32)
                for c in range(gw // 128):
                    x = pad_s[pl.ds(r0, ext), c * 128:(c + 1) * 128]
                    s = x
                    d = 1
                    while d < w:
                        s = s + pltpu.roll(s, d, 0)
                        d *= 2
                    y = s[POOL_HALO:] / cnt - x[POOL_HALO:]
                    y_s[pl.ds(r0, rows), c * 128:(c + 1) * 128] = y.astype(BF16)
                return carry

            lax.fori_loop(0, seq // rows, body, 0)

    out = jnp.dot(y_s[...], w_ref[0].astype(BF16), preferred_element_type=F32) * sc_ref[...]
    o_ref[...] = out.astype(o_ref.dtype)


def _pool(proj, pool_w, pool_scale, batch, seq, rows=128):
    m = proj.shape[0]
    gw = POOL_GROUP_WIDTH
    ng = len(POOL_WINDOWS)
    return pl.pallas_call(
        functools.partial(_pool_kernel, seq=seq, rows=rows),
        grid=(batch, ng),
        in_specs=[pl.BlockSpec((seq, gw), lambda b, g: (b, g)),
                  pl.BlockSpec((1, gw, gw), lambda b, g: (g, 0, 0)),
                  pl.BlockSpec((1, gw), lambda b, g: (0, g))],
        out_specs=pl.BlockSpec((seq, gw), lambda b, g: (b, g)),
        out_shape=jax.ShapeDtypeStruct((m, POOL_WIDTH), BF16),
        scratch_shapes=[pltpu.VMEM((seq + POOL_HALO, gw), F32), pltpu.VMEM((seq, gw), BF16)],
        compiler_params=_params(2),
        name="multiscale_pool",
    )(proj, pool_w, pool_scale)


def _rope_tables(seq):
    pos = np.arange(seq, dtype=np.float32)
    inv_freq = (np.float32(ROPE_THETA)
                ** (-np.arange(0, HEAD_DIM, 2, dtype=np.float32) / np.float32(HEAD_DIM)))
    ang = (pos[:, None] * inv_freq[None, :]).astype(np.float32)
    cos, sin = np.cos(ang).astype(np.float32), np.sin(ang).astype(np.float32)
    cos = np.concatenate([cos, cos], axis=-1)
    sin = np.concatenate([-sin, sin], axis=-1)
    scale = np.float32(HEAD_DIM ** -0.5 * LOG2_E)
    return jnp.asarray(np.stack([cos * scale, cos])), jnp.asarray(np.stack([sin * scale, sin]))


def kernel(x, ffn1_pre_g, ffn1_w_gate, ffn1_w_up, ffn1_w_down, ffn1_post_g, mix_pre_g, w_in, attn_sinks, pool_w, pool_scale, w_out, mix_post_g, ffn2_pre_g, ffn2_w_gate, ffn2_w_up, ffn2_w_down, ffn2_post_g):
    b, s, d = x.shape
    depth = ffn1_pre_g.shape[0]
    cos2, sin2 = _rope_tables(s)
    vec = lambda g: g.reshape(1, -1)
    xf = x.reshape(b * s, d)
    h = _prenorm(xf, vec(ffn1_pre_g[0]))
    for l in range(depth):
        side = [(w_in[l], SIDE_CAST_ROWS, ATTN_WIDTH + 2 * KV_WIDTH),
                (w_out[l], SIDE_CAST_ROWS, d)]
        y, (w_qkv, w_o) = _ffn(h, ffn1_w_gate[l], ffn1_w_up[l], ffn1_w_down[l], side)
        xf, h, qkv = _residual_qkv(xf, y, vec(ffn1_post_g[l]), vec(mix_pre_g[l]), w_qkv,
                                   cos2, sin2, s, FFN_RES_WEIGHT)
        a = _attention(qkv, attn_sinks[l], b, s)
        pm = _pool(_in_proj_pool(h, w_in[l]), pool_w[l], vec(pool_scale[l]), b, s)
        xf, h = _out_proj_residual(a, pm, xf, vec(mix_post_g[l]), vec(ffn2_pre_g[l]), w_o)

        y, _ = _ffn(h, ffn2_w_gate[l], ffn2_w_up[l], ffn2_w_down[l])
        g_next = vec(ffn1_pre_g[l + 1]) if l + 1 < depth else None
        xf, h = _residual(xf, y, vec(ffn2_post_g[l]), g_next, FFN_RES_WEIGHT)
    return xf.reshape(b, s, d)
```

```python
import functools

import jax
import jax.numpy as jnp
import numpy as np
from jax import lax
from jax.experimental import pallas as pl
from jax.experimental.pallas import tpu as pltpu

D_MODEL = 4096
HEAD_DIM = 128
N_HEADS = 16
N_KV_HEADS = 4
GQA_GROUP = N_HEADS // N_KV_HEADS
ATTN_WIDTH = N_HEADS * HEAD_DIM
KV_WIDTH = N_KV_HEADS * HEAD_DIM
BLOCK = 128
ROPE_THETA = 10000.0
POOL_WIDTH = D_MODEL - ATTN_WIDTH
POOL_WINDOWS = (2, 4, 8, 16)
POOL_GROUP_WIDTH = POOL_WIDTH // len(POOL_WINDOWS)
IN_PROJ_WIDTH = ATTN_WIDTH + 2 * KV_WIDTH + POOL_WIDTH
FFN_RES_WEIGHT = 0.5
RMS_EPS = 1e-6
LOG2_E = 1.4426950408889634
POOL_HALO = 16

BF16 = jnp.bfloat16
F32 = jnp.float32
BF16_SUBLANES = 16
SIDE_CAST_ROWS = 32

VMEM_LIMIT_BYTES = 62 * 1024 * 1024


def _params(n_axes):
    return pltpu.CompilerParams(
        dimension_semantics=("arbitrary",) * n_axes,
        vmem_limit_bytes=VMEM_LIMIT_BYTES,
    )


def _rms(x, g):
    ms = jnp.mean(x * x, axis=-1, keepdims=True)
    return x * lax.rsqrt(ms + RMS_EPS) * g


def _norm_kernel(x_ref, g_ref, h_ref):
    h_ref[...] = _rms(x_ref[...], g_ref[...]).astype(h_ref.dtype)


def _prenorm(x, g, tm=512):
    m, d = x.shape
    return pl.pallas_call(
        _norm_kernel,
        grid=(m // tm,),
        in_specs=[pl.BlockSpec((tm, d), lambda i: (i, 0)),
                  pl.BlockSpec((1, d), lambda i: (0, 0))],
        out_specs=pl.BlockSpec((tm, d), lambda i: (i, 0)),
        out_shape=jax.ShapeDtypeStruct((m, d), BF16),
        compiler_params=_params(1),
        name="prenorm",
    )(x, g)


def _residual_kernel(x_ref, y_ref, gpost_ref, *rest, weight, with_next):
    if with_next:
        gnext_ref, xo_ref, h_ref = rest
    else:
        (xo_ref,) = rest
    r = _rms(y_ref[...], gpost_ref[...])
    if weight != 1.0:
        r = weight * r
    xn = x_ref[...] + r
    xo_ref[...] = xn
    if with_next:
        h_ref[...] = _rms(xn, gnext_ref[...]).astype(h_ref.dtype)


def _residual(x, y, g_post, g_next, weight, tm=256):
    m, d = x.shape
    with_next = g_next is not None
    row = pl.BlockSpec((tm, d), lambda i: (i, 0))
    vec = pl.BlockSpec((1, d), lambda i: (0, 0))
    in_specs = [row, row, vec] + ([vec] if with_next else [])
    args = (x, y, g_post) + ((g_next,) if with_next else ())
    out_specs = [row] + ([row] if with_next else [])
    out_shape = [jax.ShapeDtypeStruct((m, d), F32)] + (
        [jax.ShapeDtypeStruct((m, d), BF16)] if with_next else [])
    outs = pl.pallas_call(
        functools.partial(_residual_kernel, weight=weight, with_next=with_next),
        grid=(m // tm,),
        in_specs=in_specs,
        out_specs=out_specs,
        out_shape=out_shape,
        compiler_params=_params(1),
        name="residual_norm",
    )(*args)
    return (outs[0], outs[1]) if with_next else (outs[0], None)


def _staged_kernel(*refs, n_x, n_w, side_steps, nj, slab, epilogue):
    n_side = len(side_steps)
    n_in = n_x + n_w
    x_refs = refs[:n_x]
    w_refs = refs[n_x:n_in]
    side_in = refs[n_in:n_in + n_side]
    n_in += n_side
    o_ref = refs[n_in]
    side_out = refs[n_in + 1:n_in + 1 + n_side]
    w_s = refs[n_in + 1 + n_side:]
    j = pl.program_id(0)
    i = pl.program_id(1)

    for src, dst, steps in zip(side_in, side_out, side_steps):
        @pl.when(j * pl.num_programs(1) + i < steps)
        def _(src=src, dst=dst):
            dst[...] = src[...].astype(dst.dtype)

    @pl.when(j < nj)
    def _():
        rows = pl.ds(pl.multiple_of(i * slab, slab), slab)
        for w_ref, s in zip(w_refs, w_s):
            s[j % 2, rows, :] = w_ref[...].astype(BF16)

    @pl.when(j > 0)
    def _():
        slot = (j + 1) % 2
        accs = []
        for s in w_s:
            acc, k0 = None, 0
            for x_ref in x_refs:
                kx = x_ref.shape[1]
                part = jnp.dot(x_ref[...], s[slot, k0:k0 + kx, :], preferred_element_type=F32)
                acc = part if acc is None else acc + part
                k0 += kx
            accs.append(acc)
        epilogue(accs, o_ref)


def _staged_matmul(xs, ws, *, n, out_dtype, epilogue, name, tm, tn, col0=0, side_casts=()):
    m = xs[0].shape[0]
    k = sum(x.shape[1] for x in xs)
    assert all(w.shape[0] == k for w in ws) and m % tm == 0 and col0 % tn == 0 and n % tn == 0
    ni, nj = m // tm, n // tn
    assert k % (ni * BF16_SUBLANES) == 0
    slab = k // ni
    cb = col0 // tn
    side_in, side_specs, side_shapes, side_steps = [], [], [], []
    for arr, rows, cols in side_casts:
        steps = arr.shape[0] // rows
        assert arr.shape[0] % rows == 0 and rows % BF16_SUBLANES == 0 and steps <= (nj + 1) * ni
        side_in.append(arr)
        side_specs.append(pl.BlockSpec(
            (rows, cols), lambda j, i, steps=steps: (jnp.minimum(j * ni + i, steps - 1), 0)))
        side_shapes.append(jax.ShapeDtypeStruct((arr.shape[0], cols), BF16))
        side_steps.append(steps)
    row_tile = lambda j, i: jnp.where(j == 0, 0, i)
    x_specs = [pl.BlockSpec((tm, x.shape[1]), lambda j, i: (row_tile(j, i), 0)) for x in xs]
    w_spec = pl.BlockSpec((slab, tn), lambda j, i: (jnp.where(j < nj, i, ni - 1),
                                                    cb + jnp.minimum(j, nj - 1)))
    o_spec = pl.BlockSpec((tm, tn), lambda j, i: (row_tile(j, i), jnp.maximum(j - 1, 0)))
    kern = functools.partial(_staged_kernel, n_x=len(xs), n_w=len(ws),
                             side_steps=tuple(side_steps), nj=nj, slab=slab, epilogue=epilogue)
    outs = pl.pallas_call(
        kern,
        grid=(nj + 1, ni),
        in_specs=x_specs + [w_spec] * len(ws) + side_specs,
        out_specs=[o_spec] + side_specs,
        out_shape=[jax.ShapeDtypeStruct((m, n), out_dtype)] + side_shapes,
        scratch_shapes=[pltpu.VMEM((2, k, tn), BF16) for _ in ws],
        compiler_params=_params(2),
        name=name,
    )(*xs, *ws, *side_in)
    return outs


def _store_epilogue(accs, o_ref):
    o_ref[...] = accs[0].astype(o_ref.dtype)


def _swiglu_epilogue(accs, o_ref):
    g, u = accs
    o_ref[...] = (jax.nn.silu(g) * u).astype(o_ref.dtype)


def _rope(t, cos, sin_signed):
    return t * cos + pltpu.roll(t, HEAD_DIM // 2, 1) * sin_signed


def _ffn(h, w_gate, w_up, w_down, side_casts=()):
    d_ff = w_gate.shape[1]
    tm, tn, tn_tail = 1024, 512, 256
    n_main = d_ff // tn * tn
    a, *copies = _staged_matmul([h], [w_gate, w_up], n=n_main, out_dtype=BF16,
                                epilogue=_swiglu_epilogue, name="ffn_gate_up", tm=tm, tn=tn,
                                side_casts=side_casts)
    parts = [a]
    if n_main < d_ff:
        parts += _staged_matmul([h], [w_gate, w_up], n=d_ff - n_main, col0=n_main,
                                out_dtype=BF16, epilogue=_swiglu_epilogue,
                                name="ffn_gate_up_tail", tm=2 * tm, tn=tn_tail)
    y, = _staged_matmul(parts, [w_down], n=w_down.shape[1], out_dtype=F32,
                        epilogue=_store_epilogue, name="ffn_down", tm=512, tn=512)
    return y, copies


def _in_proj_pool(h, w, tm=1024, tn=1024):
    return _staged_matmul([h], [w], n=POOL_WIDTH, out_dtype=F32, epilogue=_store_epilogue,
                          name="mix_in_proj_pool", tm=tm, tn=tn, col0=ATTN_WIDTH + 2 * KV_WIDTH)[0]


def _resident(shape):
    return pl.BlockSpec(shape, lambda t: (0,) * len(shape), pipeline_mode=pl.Buffered(1))


def _residual_qkv_kernel(x_ref, y_ref, gpost_ref, gnext_ref, w_ref, cos_ref, sin_ref,
                         xo_ref, h_ref, qkv_ref, *, weight):
    xn = x_ref[...] + weight * _rms(y_ref[...], gpost_ref[...])
    xo_ref[...] = xn
    h = _rms(xn, gnext_ref[...]).astype(BF16)
    h_ref[...] = h
    acc = jnp.dot(h, w_ref[...], preferred_element_type=F32)
    for c in range(acc.shape[1] // HEAD_DIM):
        cols = slice(c * HEAD_DIM, (c + 1) * HEAD_DIM)
        if c * HEAD_DIM < ATTN_WIDTH + KV_WIDTH:
            kind = 0 if c * HEAD_DIM < ATTN_WIDTH else 1
            val = _rope(acc[:, cols], cos_ref[kind], sin_ref[kind])
        else:
            val = acc[:, cols]
        qkv_ref[:, cols] = val.astype(qkv_ref.dtype)


def _residual_qkv(x, y, g_post, g_next, w_qkv, cos2, sin2, seq, weight, tm=256):
    m, d = x.shape
    n = ATTN_WIDTH + 2 * KV_WIDTH
    assert w_qkv.shape == (d, n) and w_qkv.dtype == BF16 and m % tm == 0 and seq % tm == 0
    row = lambda width: pl.BlockSpec((tm, width), lambda t: (t, 0))
    vec = pl.BlockSpec((1, d), lambda t: (0, 0))
    table = pl.BlockSpec((2, tm, HEAD_DIM), lambda t: (0, t % (seq // tm), 0))
    return pl.pallas_call(
        functools.partial(_residual_qkv_kernel, weight=weight),
        grid=(m // tm,),
        in_specs=[row(d), row(d), vec, vec, _resident((d, n)), table, table],
        out_specs=[row(d), row(d), row(n)],
        out_shape=[jax.ShapeDtypeStruct((m, d), F32), jax.ShapeDtypeStruct((m, d), BF16),
                   jax.ShapeDtypeStruct((m, n), BF16)],
        compiler_params=_params(1),
        name="residual_qkv",
    )(x, y, g_post, g_next, w_qkv, cos2, sin2)


def _out_proj_residual_kernel(a_ref, p_ref, x_ref, gpost_ref, gnext_ref, w_ref, xo_ref, h_ref):
    ka = a_ref.shape[1]
    y = (jnp.dot(a_ref[...], w_ref[0:ka, :], preferred_element_type=F32)
         + jnp.dot(p_ref[...], w_ref[ka:, :], preferred_element_type=F32))
    xn = x_ref[...] + _rms(y, gpost_ref[...])
    xo_ref[...] = xn
    h_ref[...] = _rms(xn, gnext_ref[...]).astype(h_ref.dtype)


def _out_proj_residual(a, pm, x, g_post, g_next, w_bf16, tm=256):
    m, d = x.shape
    k = a.shape[1] + pm.shape[1]
    assert w_bf16.shape == (k, d) and w_bf16.dtype == BF16 and m % tm == 0
    row = lambda width: pl.BlockSpec((tm, width), lambda t: (t, 0))
    vec = pl.BlockSpec((1, d), lambda t: (0, 0))
    return pl.pallas_call(
        _out_proj_residual_kernel,
        grid=(m // tm,),
        in_specs=[row(a.shape[1]), row(pm.shape[1]), row(d), vec, vec, _resident((k, d))],
        out_specs=[row(d), row(d)],
        out_shape=[jax.ShapeDtypeStruct((m, d), F32), jax.ShapeDtypeStruct((m, d), BF16)],
        compiler_params=_params(1),
        name="out_proj_residual",
    )(a, pm, x, g_post, g_next, w_bf16)


def _attn_kernel(sinks_ref, q_ref, k_ref, v_ref, o_ref, bias_s, *, seq):
    kvh = pl.program_id(1)
    g, blk = GQA_GROUP, BLOCK
    key = lax.broadcasted_iota(jnp.int32, (2 * blk, blk), 0)
    qry = lax.broadcasted_iota(jnp.int32, (2 * blk, blk), 1)
    cur_ok = (key >= blk) & (key - blk <= qry)
    prev_ok = (key < blk) & (key > qry)
    bias_s[0] = jnp.where(cur_ok, 0.0, -jnp.inf)
    bias_s[1] = jnp.where(cur_ok | prev_ok, 0.0, -jnp.inf)
    sinks = [jnp.full((1, blk), sinks_ref[kvh * g + c] * LOG2_E, F32) for c in range(g)]
    contract_last = (((1,), (1,)), ((), ()))
    contract_first = (((0,), (0,)), ((), ()))

    def body(n, carry):
        r0 = pl.multiple_of(n * blk, blk)
        rp = pl.multiple_of(jnp.maximum(n - 1, 0) * blk, blk)
        kw = jnp.concatenate([k_ref[pl.ds(rp, blk), :], k_ref[pl.ds(r0, blk), :]], axis=0)
        vw = jnp.concatenate([v_ref[pl.ds(rp, blk), :], v_ref[pl.ds(r0, blk), :]], axis=0)
        bias = bias_s[jnp.minimum(n, 1)]
        for c in range(g):
            qc = q_ref[pl.ds(r0, blk), c * HEAD_DIM:(c + 1) * HEAD_DIM]
            s = lax.dot_general(kw, qc, contract_last, preferred_element_type=F32) + bias
            m = jnp.maximum(jnp.max(s, axis=0, keepdims=True), sinks[c])
            e = jnp.exp2(s - m)
            denom = jnp.sum(e, axis=0, keepdims=True) + jnp.exp2(sinks[c] - m)
            p = (e * (1.0 / denom)).astype(BF16)
            o = lax.dot_general(p, vw, contract_first, preferred_element_type=F32)
            o_ref[pl.ds(r0, blk), c * HEAD_DIM:(c + 1) * HEAD_DIM] = o.astype(o_ref.dtype)
        return carry

    lax.fori_loop(0, seq // blk, body, 0, unroll=16)


def _attention(qkv, sinks, batch, seq):
    m = qkv.shape[0]
    qw = GQA_GROUP * HEAD_DIM
    k_blk0 = ATTN_WIDTH // HEAD_DIM
    v_blk0 = (ATTN_WIDTH + KV_WIDTH) // HEAD_DIM
    return pl.pallas_call(
        functools.partial(_attn_kernel, seq=seq),
        grid=(batch, N_KV_HEADS),
        in_specs=[pl.BlockSpec(memory_space=pltpu.SMEM),
                  pl.BlockSpec((seq, qw), lambda b, h: (b, h)),
                  pl.BlockSpec((seq, HEAD_DIM), lambda b, h: (b, k_blk0 + h)),
                  pl.BlockSpec((seq, HEAD_DIM), lambda b, h: (b, v_blk0 + h))],
        out_specs=pl.BlockSpec((seq, qw), lambda b, h: (b, h)),
        out_shape=jax.ShapeDtypeStruct((m, ATTN_WIDTH), BF16),
        scratch_shapes=[pltpu.VMEM((2, 2 * BLOCK, BLOCK), F32)],
        compiler_params=_params(2),
        name="swa_attention",
    )(sinks, qkv, qkv, qkv)


def _pool_kernel(p_ref, w_ref, sc_ref, o_ref, y_s, *, seq, rows):
    grp = pl.program_id(1)
    gw = POOL_GROUP_WIDTH
    ext = rows + POOL_HALO
    lrow = lax.broadcasted_iota(jnp.int32, (rows, 128), 0)
    no_history = jnp.zeros((POOL_HALO, 128), F32)

    for gi, w in enumerate(POOL_WINDOWS):
        assert w & (w - 1) == 0 and w - 1 <= POOL_HALO

        @pl.when(grp == gi)
        def _(w=w):
            def chunk(x, r0, c):
                cnt = jnp.minimum(lrow + (r0 + 1), w).astype(F32)
                s = x
                d = 1
                while d < w:
                    s = s + pltpu.roll(s, d, 0)
                    d *= 2
                y = s[POOL_HALO:] / cnt - x[POOL_HALO:]
                y_s[pl.ds(r0, rows), c * 128:(c + 1) * 128] = y.astype(BF16)

            for c in range(gw // 128):
                cols = slice(c * 128, (c + 1) * 128)
                chunk(jnp.concatenate([no_history, p_ref[0:rows, cols]], axis=0), 0, c)

            def body(r, carry):
                r0 = pl.multiple_of(r * rows, rows)
                for c in range(gw // 128):
                    chunk(p_ref[pl.ds(r0 - POOL_HALO, ext), c * 128:(c + 1) * 128], r0, c)
                return carry

            lax.fori_loop(1, seq // rows, body, 0)

    out = jnp.dot(y_s[...], w_ref[0].astype(BF16), preferred_element_type=F32) * sc_ref[...]
    o_ref[...] = out.astype(o_ref.dtype)


def _pool(proj, pool_w, pool_scale, batch, seq, rows=128):
    m = proj.shape[0]
    gw = POOL_GROUP_WIDTH
    ng = len(POOL_WINDOWS)
    return pl.pallas_call(
        functools.partial(_pool_kernel, seq=seq, rows=rows),
        grid=(batch, ng),
        in_specs=[pl.BlockSpec((seq, gw), lambda b, g: (b, g)),
                  pl.BlockSpec((1, gw, gw), lambda b, g: (g, 0, 0)),
                  pl.BlockSpec((1, gw), lambda b, g: (0, g))],
        out_specs=pl.BlockSpec((seq, gw), lambda b, g: (b, g)),
        out_shape=jax.ShapeDtypeStruct((m, POOL_WIDTH), BF16),
        scratch_shapes=[pltpu.VMEM((seq, gw), BF16)],
        compiler_params=_params(2),
        name="multiscale_pool",
    )(proj, pool_w, pool_scale)


def _rope_tables(seq):
    pos = np.arange(seq, dtype=np.float32)
    inv_freq = (np.float32(ROPE_THETA)
                ** (-np.arange(0, HEAD_DIM, 2, dtype=np.float32) / np.float32(HEAD_DIM)))
    ang = (pos[:, None] * inv_freq[None, :]).astype(np.float32)
    cos, sin = np.cos(ang).astype(np.float32), np.sin(ang).astype(np.float32)
    cos = np.concatenate([cos, cos], axis=-1)
    sin = np.concatenate([-sin, sin], axis=-1)
    scale = np.float32(HEAD_DIM ** -0.5 * LOG2_E)
    return jnp.asarray(np.stack([cos * scale, cos])), jnp.asarray(np.stack([sin * scale, sin]))


def kernel(x, ffn1_pre_g, ffn1_w_gate, ffn1_w_up, ffn1_w_down, ffn1_post_g, mix_pre_g, w_in, attn_sinks, pool_w, pool_scale, w_out, mix_post_g, ffn2_pre_g, ffn2_w_gate, ffn2_w_up, ffn2_w_down, ffn2_post_g):
    b, s, d = x.shape
    depth = ffn1_pre_g.shape[0]
    cos2, sin2 = _rope_tables(s)
    vec = lambda g: g.reshape(1, -1)
    xf = x.reshape(b * s, d)
    h = _prenorm(xf, vec(ffn1_pre_g[0]))
    for l in range(depth):
        side = [(w_in[l], SIDE_CAST_ROWS, ATTN_WIDTH + 2 * KV_WIDTH),
                (w_out[l], SIDE_CAST_ROWS, d)]
        y, (w_qkv, w_o) = _ffn(h, ffn1_w_gate[l], ffn1_w_up[l], ffn1_w_down[l], side)
        xf, h, qkv = _residual_qkv(xf, y, vec(ffn1_post_g[l]), vec(mix_pre_g[l]), w_qkv,
                                   cos2, sin2, s, FFN_RES_WEIGHT)
        a = _attention(qkv, attn_sinks[l], b, s)
        pm = _pool(_in_proj_pool(h, w_in[l]), pool_w[l], vec(pool_scale[l]), b, s)
        xf, h = _out_proj_residual(a, pm, xf, vec(mix_post_g[l]), vec(ffn2_pre_g[l]), w_o)

        y, _ = _ffn(h, ffn2_w_gate[l], ffn2_w_up[l], ffn2_w_down[l])
        g_next = vec(ffn1_pre_g[l + 1]) if l + 1 < depth else None
        xf, h = _residual(xf, y, vec(ffn2_post_g[l]), g_next, FFN_RES_WEIGHT)
    return xf.reshape(b, s, d)
```
